```python
import jax, jax.numpy as jnp
from jax import lax
import numpy as np

D_MODEL = 4096
BATCH = 2
SEQ = 4096
DEPTH = 1
DEC_BATCH = 16
DEC_SEQ = 32
PAST_LEN = 1024

CHUNK = 64
D_MIX = D_MODEL
MLA_HEADS = 16
QK_NOPE_DIM = 128
ROPE_DIM = 64
V_HEAD_DIM = 128
Q_LORA_RANK = 768
KV_LORA_RANK = 512
ROPE_THETA = 10000.0
ATTN_SCALE = (QK_NOPE_DIM + ROPE_DIM) ** -0.5
Q_BLOCK = 128
MLA_WIDTH = MLA_HEADS * V_HEAD_DIM
GMLP_GROUPS = 16
GMLP_GROUP_DIM = (D_MIX - MLA_WIDTH) // GMLP_GROUPS
GMLP_WIDTH = GMLP_GROUPS * GMLP_GROUP_DIM
GMLP_CHUNK = 128
COL_Q = Q_LORA_RANK
COL_KV = COL_Q + KV_LORA_RANK
COL_KR = COL_KV + ROPE_DIM
IN_COLS = COL_KR + 2 * GMLP_WIDTH
N_EXPERTS = 32
TOP_K = 4
D_EXPERT = D_MODEL
SWIGLU_ALPHA = 1.702
SWIGLU_LIMIT = 7.0
MOE_BLOCK = 128
LN_EPS = 1e-5
RMS_EPS = 1e-6
DEEPNORM_ALPHA = (2 * DEPTH) ** 0.25
DEEPNORM_BETA = (8 * DEPTH) ** -0.25

kernel_name = 'hybrid_gmlp_mla_moe_deepnorm_stream_step'


def layer_norm(x, g, b):
    xf = x.astype(jnp.float32)
    mu = jnp.mean(xf, axis=-1, keepdims=True)
    var = jnp.mean(jnp.square(xf - mu), axis=-1, keepdims=True)
    y = (xf - mu) * lax.rsqrt(var + LN_EPS) * g.astype(jnp.float32) + b.astype(jnp.float32)
    return y.astype(x.dtype)


def rms_norm(x, g):
    xf = x.astype(jnp.float32)
    y = xf * lax.rsqrt(jnp.mean(jnp.square(xf), axis=-1, keepdims=True) + RMS_EPS) * g.astype(jnp.float32)
    return y.astype(x.dtype)


def rope_tables(pos):
    half = ROPE_DIM // 2
    inv_freq = ROPE_THETA ** (-jnp.arange(half, dtype=jnp.float32) / half)
    ang = pos.astype(jnp.float32)[:, None] * inv_freq[None, :]
    return jnp.cos(ang), jnp.sin(ang)


def apply_rope(x, cos, sin):
    half = ROPE_DIM // 2
    x1 = x[..., :half].astype(jnp.float32)
    x2 = x[..., half:].astype(jnp.float32)
    return jnp.concatenate([x1 * cos - x2 * sin, x2 * cos + x1 * sin], axis=-1).astype(x.dtype)


def mixer_inputs(x, w_in, q_a_norm_g, kv_a_norm_g, w_q_b, pos):
    proj = jnp.einsum('bsd,dc->bsc', x, w_in)
    c_q = rms_norm(proj[..., :COL_Q], q_a_norm_g)
    c_kv = rms_norm(proj[..., COL_Q:COL_KV], kv_a_norm_g)
    k_pe = proj[..., COL_KV:COL_KR]
    z = jax.nn.gelu(proj[..., COL_KR:], approximate=False)
    q = jnp.einsum('bsr,rhe->bshe', c_q, w_q_b)
    q_nope, q_pe = q[..., :QK_NOPE_DIM], q[..., QK_NOPE_DIM:]
    cos, sin = rope_tables(pos)
    q_pe = apply_rope(q_pe, cos[:, None, :], sin[:, None, :])
    k_pe = apply_rope(k_pe, cos, sin)
    return q_nope, q_pe, c_kv, k_pe, z


def mla_attend(q_nope, q_pe, c_kv, k_pe, q_pos, k_pos, w_uk, w_uv):
    q_lat = jnp.einsum('bqhd,chd->bqhc', q_nope, w_uk)
    s = jnp.einsum('bqhc,bkc->bhqk', q_lat, c_kv) + jnp.einsum('bqhr,bkr->bhqk', q_pe, k_pe)
    s = s.astype(jnp.float32) * ATTN_SCALE
    allowed = (k_pos[None, :] // CHUNK) <= (q_pos[:, None] // CHUNK)
    s = jnp.where(allowed[None, None], s, -jnp.inf)
    p = jax.nn.softmax(s, axis=-1).astype(c_kv.dtype)
    o_lat = jnp.einsum('bhqk,bkc->bqhc', p, c_kv)
    o = jnp.einsum('bqhc,chd->bqhd', o_lat, w_uv)
    return o.reshape(o.shape[0], o.shape[1], MLA_WIDTH)


def mla_prompt(q_nope, q_pe, c_kv, k_pe, pos, w_uk, w_uv):
    b, s = q_nope.shape[:2]
    nb = s // Q_BLOCK

    def blocks(a):
        return jnp.moveaxis(a.reshape(b, nb, Q_BLOCK, *a.shape[2:]), 1, 0)

    def one_block(args):
        qn, qr, qp = args
        return mla_attend(qn, qr, c_kv, k_pe, qp, pos, w_uk, w_uv)

    out = lax.map(one_block, (blocks(q_nope), blocks(q_pe), pos.reshape(nb, Q_BLOCK)))
    return jnp.moveaxis(out, 0, 1).reshape(b, s, MLA_WIDTH)


def gmlp_spatial_gate(z, norm_g, norm_b, w_s, b_s, chunk_len):
    b, s, _ = z.shape
    u = z[..., :GMLP_WIDTH]
    v = layer_norm(z[..., GMLP_WIDTH:], norm_g, norm_b).reshape(b, s, GMLP_GROUPS, GMLP_GROUP_DIM)
    pos = jnp.arange(chunk_len)
    mask = (pos[None, :] // CHUNK) <= (pos[:, None] // CHUNK)
    w = jnp.where(mask[None], w_s[:, :chunk_len, :chunk_len], 0.0)
    vc = v.reshape(b, s // chunk_len, chunk_len, GMLP_GROUPS, GMLP_GROUP_DIM)
    mix = jnp.einsum('gij,bcjgd->bcigd', w, vc) + b_s[:, :chunk_len].T[None, None, :, :, None]
    return u * mix.reshape(b, s, GMLP_WIDTH), v


def moe_ffn(x2d, w_router, b_router, w_gate_up, b_gate_up, w_down, b_down):
    n_tok, d = x2d.shape
    logits = jnp.einsum('td,de->te', x2d, w_router).astype(jnp.float32) + b_router.astype(jnp.float32)
    top_logit, top_idx = lax.top_k(logits, TOP_K)
    gates = jax.nn.softmax(top_logit, axis=-1)
    n_assign = n_tok * TOP_K
    flat_e = top_idx.reshape(-1).astype(jnp.int32)
    order = jnp.argsort(flat_e)
    sorted_e = flat_e[order]
    sorted_tok = (order // TOP_K).astype(jnp.int32)
    sorted_gate = gates.reshape(-1)[order]
    counts = jnp.bincount(flat_e, length=N_EXPERTS)
    padded = (counts + MOE_BLOCK - 1) // MOE_BLOCK * MOE_BLOCK
    pad_end = jnp.cumsum(padded)
    pad_start = pad_end - padded
    start = jnp.cumsum(counts) - counts
    dest = pad_start[sorted_e] + jnp.arange(n_assign, dtype=jnp.int32) - start[sorted_e]
    n_blocks = (n_assign + MOE_BLOCK - 1) // MOE_BLOCK + N_EXPERTS
    n_slots = n_blocks * MOE_BLOCK
    slot_tok = jnp.full((n_slots,), n_tok, dtype=jnp.int32).at[dest].set(sorted_tok)
    block_expert = jnp.minimum(
        jnp.searchsorted(pad_end, jnp.arange(n_blocks, dtype=jnp.int32) * MOE_BLOCK, side='right'),
        N_EXPERTS - 1).astype(jnp.int32)
    x_pad = jnp.concatenate([x2d, jnp.zeros((1, d), x2d.dtype)], axis=0)

    def expert_block(args):
        tok, e = args
        xb = x_pad[tok]
        h = xb @ w_gate_up[e] + b_gate_up[e]
        gate = jnp.minimum(h[:, :D_EXPERT], SWIGLU_LIMIT)
        lin = jnp.clip(h[:, D_EXPERT:], -SWIGLU_LIMIT, SWIGLU_LIMIT)
        act = gate * jax.nn.sigmoid(SWIGLU_ALPHA * gate) * (lin + 1.0)
        return act @ w_down[e] + b_down[e]

    y_slots = lax.map(expert_block, (slot_tok.reshape(n_blocks, MOE_BLOCK), block_expert))
    y_assign = y_slots.reshape(n_slots, d)[dest] * sorted_gate[:, None].astype(x2d.dtype)
    return jax.ops.segment_sum(y_assign, sorted_tok, num_segments=n_tok)


def setup_inputs(seed: int = 0) -> dict:
    key = jax.random.key(seed)
    ks = jax.random.split(key, 24)
    f32 = jnp.float32
    nrm = lambda k, shape, scale: jax.random.normal(k, shape, f32) * scale
    gain = lambda k, shape: 1.0 + 0.05 * jax.random.normal(k, shape, f32)
    L = DEPTH
    return {
        'x_prompt': nrm(ks[0], (BATCH, SEQ, D_MODEL), 1.0),
        'x_sample': nrm(ks[1], (DEC_BATCH, DEC_SEQ, D_MODEL), 1.0),
        'cache_kv_latent': nrm(ks[2], (L, DEC_BATCH, PAST_LEN, KV_LORA_RANK), 1.0),
        'cache_k_rope': nrm(ks[3], (L, DEC_BATCH, PAST_LEN, ROPE_DIM), 1.0),
        'w_in': nrm(ks[4], (L, D_MODEL, IN_COLS), D_MODEL ** -0.5),
        'q_a_norm_g': gain(ks[5], (L, Q_LORA_RANK)),
        'kv_a_norm_g': gain(ks[6], (L, KV_LORA_RANK)),
        'w_q_b': nrm(ks[7], (L, Q_LORA_RANK, MLA_HEADS, QK_NOPE_DIM + ROPE_DIM), Q_LORA_RANK ** -0.5),
        'w_uk': nrm(ks[8], (L, KV_LORA_RANK, MLA_HEADS, QK_NOPE_DIM), KV_LORA_RANK ** -0.5),
        'w_uv': nrm(ks[9], (L, KV_LORA_RANK, MLA_HEADS, V_HEAD_DIM), KV_LORA_RANK ** -0.5),
        'gmlp_norm_g': gain(ks[10], (L, GMLP_WIDTH)),
        'gmlp_norm_b': nrm(ks[11], (L, GMLP_WIDTH), 0.02),
        'gmlp_w_s': nrm(ks[12], (L, GMLP_GROUPS, GMLP_CHUNK, GMLP_CHUNK), GMLP_CHUNK ** -0.5),
        'gmlp_b_s': gain(ks[13], (L, GMLP_GROUPS, GMLP_CHUNK)),
        'w_o': nrm(ks[14], (L, D_MIX, D_MODEL), D_MIX ** -0.5 * DEEPNORM_BETA),
        'ln1_g': gain(ks[15], (L, D_MODEL)),
        'ln1_b': nrm(ks[16], (L, D_MODEL), 0.02),
        'w_router': nrm(ks[17], (L, D_MODEL, N_EXPERTS), D_MODEL ** -0.5),
        'b_router': nrm(ks[18], (L, N_EXPERTS), 0.01),
        'w_gate_up': nrm(ks[19], (L, N_EXPERTS, D_MODEL, 2 * D_EXPERT), D_MODEL ** -0.5),
        'b_gate_up': nrm(ks[20], (L, N_EXPERTS, 2 * D_EXPERT), 0.01),
        'w_down': nrm(ks[21], (L, N_EXPERTS, D_EXPERT, D_MODEL), D_EXPERT ** -0.5 * DEEPNORM_BETA),
        'b_down': nrm(ks[22], (L, N_EXPERTS, D_MODEL), 0.01),
        'ln2_g': gain(ks[23], (L, D_MODEL)),
        'ln2_b': nrm(jax.random.fold_in(ks[23], 1), (L, D_MODEL), 0.02),
    }


def reference(x_prompt, x_sample, cache_kv_latent, cache_k_rope, w_in, q_a_norm_g, kv_a_norm_g, w_q_b,
              w_uk, w_uv, gmlp_norm_g, gmlp_norm_b, gmlp_w_s, gmlp_b_s, w_o, ln1_g, ln1_b, w_router,
              b_router, w_gate_up, b_gate_up, w_down, b_down, ln2_g, ln2_b):
    pos_p = jnp.arange(SEQ, dtype=jnp.int32)
    pos_s = PAST_LEN + jnp.arange(DEC_SEQ, dtype=jnp.int32)
    key_pos_s = jnp.arange(PAST_LEN + DEC_SEQ, dtype=jnp.int32)
    x_p, x_s = x_prompt, x_sample
    lat_p, kr_p_all, lat_s, kr_s_all, gv_s_all = [], [], [], [], []
    for l in range(DEPTH):
        qn_p, qr_p, ckv_p, kr_p, z_p = mixer_inputs(x_p, w_in[l], q_a_norm_g[l], kv_a_norm_g[l], w_q_b[l], pos_p)
        qn_s, qr_s, ckv_s, kr_s, z_s = mixer_inputs(x_s, w_in[l], q_a_norm_g[l], kv_a_norm_g[l], w_q_b[l], pos_s)
        att_p = mla_prompt(qn_p, qr_p, ckv_p, kr_p, pos_p, w_uk[l], w_uv[l])
        keys_lat = jnp.concatenate([cache_kv_latent[l].astype(ckv_s.dtype), ckv_s], axis=1)
        keys_rope = jnp.concatenate([cache_k_rope[l].astype(kr_s.dtype), kr_s], axis=1)
        att_s = mla_attend(qn_s, qr_s, keys_lat, keys_rope, pos_s, key_pos_s, w_uk[l], w_uv[l])
        g_p, _ = gmlp_spatial_gate(z_p, gmlp_norm_g[l], gmlp_norm_b[l], gmlp_w_s[l], gmlp_b_s[l], GMLP_CHUNK)
        g_s, v_s = gmlp_spatial_gate(z_s, gmlp_norm_g[l], gmlp_norm_b[l], gmlp_w_s[l], gmlp_b_s[l], DEC_SEQ)
        mix_p = jnp.einsum('bsc,cd->bsd', jnp.concatenate([att_p, g_p], axis=-1), w_o[l])
        mix_s = jnp.einsum('bsc,cd->bsd', jnp.concatenate([att_s, g_s], axis=-1), w_o[l])
        x_p = layer_norm(DEEPNORM_ALPHA * x_p + mix_p, ln1_g[l], ln1_b[l])
        x_s = layer_norm(DEEPNORM_ALPHA * x_s + mix_s, ln1_g[l], ln1_b[l])
        n_p = BATCH * SEQ
        tok = jnp.concatenate([x_p.reshape(n_p, D_MODEL), x_s.reshape(DEC_BATCH * DEC_SEQ, D_MODEL)], axis=0)
        f = moe_ffn(tok, w_router[l], b_router[l], w_gate_up[l], b_gate_up[l], w_down[l], b_down[l])
        x_p = layer_norm(DEEPNORM_ALPHA * x_p + f[:n_p].reshape(BATCH, SEQ, D_MODEL), ln2_g[l], ln2_b[l])
        x_s = layer_norm(DEEPNORM_ALPHA * x_s + f[n_p:].reshape(DEC_BATCH, DEC_SEQ, D_MODEL), ln2_g[l], ln2_b[l])
        lat_p.append(ckv_p)
        kr_p_all.append(kr_p)
        lat_s.append(ckv_s)
        kr_s_all.append(kr_s)
        gv_s_all.append(v_s)
    new_kv_latent_prompt = jnp.stack(lat_p, axis=0)
    new_k_rope_prompt = jnp.stack(kr_p_all, axis=0)
    new_kv_latent_sample = jnp.stack(lat_s, axis=0)
    new_k_rope_sample = jnp.stack(kr_s_all, axis=0)
    new_gmlp_v_sample = jnp.stack(gv_s_all, axis=0)
    return (x_p, x_s, new_kv_latent_prompt, new_k_rope_prompt, new_kv_latent_sample, new_k_rope_sample, new_gmlp_v_sample)
```

```python
import functools

import jax
import jax.numpy as jnp
from jax import lax
from jax.experimental import pallas as pl
from jax.experimental.pallas import tpu as pltpu

F32 = jnp.float32
BF16 = jnp.bfloat16

CHUNK = 64
MLA_HEADS = 16
QK_NOPE_DIM = 128
ROPE_DIM = 64
V_HEAD_DIM = 128
Q_LORA_RANK = 768
KV_LORA_RANK = 512
ROPE_THETA = 10000.0
ATTN_SCALE = (QK_NOPE_DIM + ROPE_DIM) ** -0.5
GMLP_GROUPS = 16
GMLP_GROUP_DIM = 128
GMLP_WIDTH = GMLP_GROUPS * GMLP_GROUP_DIM
GMLP_CHUNK = 128
MLA_WIDTH = MLA_HEADS * V_HEAD_DIM
COL_Q = Q_LORA_RANK
COL_KV = COL_Q + KV_LORA_RANK
COL_KR = COL_KV + ROPE_DIM
N_EXPERTS = 32
TOP_K = 4
SWIGLU_ALPHA = 1.702
SWIGLU_LIMIT = 7.0
LN_EPS = 1e-5
RMS_EPS = 1e-6

LANES = 128
QK_PAD = 256
KV_IN = KV_LORA_RANK + LANES
KV_OUT = MLA_HEADS * QK_PAD + MLA_HEADS * V_HEAD_DIM
MOE_ROWS = 1280
MOE_SUB = 256
MOE_TN = 256
VMEM_LIMIT = 56 * 1024 * 1024


def _params(*sem):
    return pltpu.CompilerParams(dimension_semantics=sem, vmem_limit_bytes=VMEM_LIMIT)


def _rms(x, g):
    return x * lax.rsqrt(jnp.mean(x * x, axis=-1, keepdims=True) + RMS_EPS) * g


def _ln(x, g, b):
    mu = jnp.mean(x, axis=-1, keepdims=True)
    d = x - mu
    var = jnp.mean(d * d, axis=-1, keepdims=True)
    return d * lax.rsqrt(var + LN_EPS) * g + b


def _inproj_lat_kernel(x_ref, w_ref, gq_ref, gkv_ref, cs_ref, cq_ref, ckv_ref, kpe_ref, a_ref):
    p = jnp.dot(x_ref[...], w_ref[...], preferred_element_type=F32)
    cq_ref[...] = _rms(p[:, :COL_Q], gq_ref[...]).astype(BF16)
    ckv = _rms(p[:, COL_Q:COL_KV], gkv_ref[...])
    ckv_ref[...] = ckv
    t = p[:, COL_KV:] * cs_ref[...]
    r = t + pltpu.roll(t, ROPE_DIM, axis=1)
    kpe_ref[...] = r[:, :ROPE_DIM]
    a_ref[:, :KV_LORA_RANK] = ckv.astype(BF16)
    a_ref[:, KV_LORA_RANK:] = r.astype(BF16)


def _inproj_lat(x_bf, w_lat, gq, gkv, kcs, bm):
    t, d = x_bf.shape
    n = w_lat.shape[1]
    return pl.pallas_call(
        _inproj_lat_kernel,
        grid=(t // bm,),
        in_specs=[
            pl.BlockSpec((bm, d), lambda i: (i, 0)),
            pl.BlockSpec((d, n), lambda i: (0, 0)),
            pl.BlockSpec((1, COL_Q), lambda i: (0, 0)),
            pl.BlockSpec((1, KV_LORA_RANK), lambda i: (0, 0)),
            pl.BlockSpec((bm, LANES), lambda i: (i, 0)),
        ],
        out_specs=[
            pl.BlockSpec((bm, COL_Q), lambda i: (i, 0)),
            pl.BlockSpec((bm, KV_LORA_RANK), lambda i: (i, 0)),
            pl.BlockSpec((bm, ROPE_DIM), lambda i: (i, 0)),
            pl.BlockSpec((bm, KV_IN), lambda i: (i, 0)),
        ],
        out_shape=[
            jax.ShapeDtypeStruct((t, COL_Q), BF16),
            jax.ShapeDtypeStruct((t, KV_LORA_RANK), F32),
            jax.ShapeDtypeStruct((t, ROPE_DIM), F32),
            jax.ShapeDtypeStruct((t, KV_IN), BF16),
        ],
        compiler_params=_params("parallel"),
        name="inproj_lat",
    )(x_bf, w_lat, gq, gkv, kcs)


def _inproj_gelu_kernel(x_ref, w_ref, o_ref):
    p = jnp.dot(x_ref[...], w_ref[...], preferred_element_type=F32)
    o_ref[...] = (0.5 * p * (1.0 + lax.erf(p * (2.0 ** -0.5)))).astype(o_ref.dtype)


def _inproj_gelu(x_bf, w, out_dtype, bm, bn):
    t, d = x_bf.shape
    n = w.shape[1]
    return pl.pallas_call(
        _inproj_gelu_kernel,
        grid=(t // bm, n // bn),
        in_specs=[
            pl.BlockSpec((bm, d), lambda i, j: (i, 0)),
            pl.BlockSpec((d, bn), lambda i, j: (0, j)),
        ],
        out_specs=pl.BlockSpec((bm, bn), lambda i, j: (i, j)),
        out_shape=jax.ShapeDtypeStruct((t, n), out_dtype),
        compiler_params=_params("parallel", "parallel"),
        name="inproj_gelu",
    )(x_bf, w)


def _qproj_kernel(c_ref, w_ref, tab_ref, o_ref):
    p = jnp.dot(c_ref[...], w_ref[...], preferred_element_type=F32) * tab_ref[...]
    o_ref[:, :QK_NOPE_DIM] = p[:, :QK_NOPE_DIM].astype(BF16)
    t = p[:, QK_NOPE_DIM:]
    o_ref[:, QK_NOPE_DIM:] = (t + pltpu.roll(t, ROPE_DIM, axis=1)).astype(BF16)


def _qproj(cq, w_q, qtab, bm):
    t, r = cq.shape
    return pl.pallas_call(
        _qproj_kernel,
        grid=(t // bm, MLA_HEADS),
        in_specs=[
            pl.BlockSpec((bm, r), lambda i, h: (i, 0)),
            pl.BlockSpec((r, QK_PAD), lambda i, h: (0, h)),
            pl.BlockSpec((bm, QK_PAD), lambda i, h: (i, 0)),
        ],
        out_specs=pl.BlockSpec((bm, QK_PAD), lambda i, h: (i, h)),
        out_shape=jax.ShapeDtypeStruct((t, MLA_HEADS * QK_PAD), BF16),
        compiler_params=_params("parallel", "parallel"),
        name="qproj",
    )(cq, w_q, qtab)


def _matmul_kernel(x_ref, w_ref, o_ref):
    o_ref[...] = jnp.dot(x_ref[...], w_ref[...], preferred_element_type=F32).astype(o_ref.dtype)


def _matmul(x, w, out_dtype, bm, bn, name):
    m, k = x.shape
    n = w.shape[1]
    return pl.pallas_call(
        _matmul_kernel,
        grid=(m // bm, n // bn),
        in_specs=[
            pl.BlockSpec((bm, k), lambda i, j: (i, 0)),
            pl.BlockSpec((k, bn), lambda i, j: (0, j)),
        ],
        out_specs=pl.BlockSpec((bm, bn), lambda i, j: (i, j)),
        out_shape=jax.ShapeDtypeStruct((m, n), out_dtype),
        compiler_params=_params("parallel", "parallel"),
        name=name,
    )(x, w)


def _attn_kernel(q_ref, k_ref, v_ref, o_ref, *, tq, tk, sk_valid, q_pos0):
    i = pl.program_id(2)
    q = q_ref[...]
    q_lo = q_pos0 + i * tq
    seen_by_all = jnp.minimum((q_lo // CHUNK + 1) * CHUNK, sk_valid)
    seen_by_any = jnp.minimum(((q_lo + tq - 1) // CHUNK + 1) * CHUNK, sk_valid)
    n_full = seen_by_all // tk
    n_all = (seen_by_any + tk - 1) // tk

    def step(masked):
        def body(j, carry):
            m, l, acc = carry
            k0 = pl.multiple_of(j * tk, tk)
            k = k_ref[pl.ds(k0, tk), :]
            s = lax.dot_general(q, k, (((1,), (1,)), ((), ())), preferred_element_type=F32)
            if masked:
                qpos = q_lo + lax.broadcasted_iota(jnp.int32, (tq, tk), 0)
                kpos = k0 + lax.broadcasted_iota(jnp.int32, (tq, tk), 1)
                ok = (kpos // CHUNK <= qpos // CHUNK) & (kpos < sk_valid)
                s = jnp.where(ok, s, -jnp.inf)
            m_new = jnp.maximum(m, jnp.max(s, axis=-1, keepdims=True))
            a = jnp.exp(m - m_new)
            p = jnp.exp(s - m_new)
            l = a * l + jnp.sum(p, axis=-1, keepdims=True)
            v = v_ref[pl.ds(k0, tk), :]
            acc = a * acc + jnp.dot(p.astype(BF16), v, preferred_element_type=F32)
            return m_new, l, acc

        return body

    init = (
        jnp.full((tq, 1), -1e30, F32),
        jnp.zeros((tq, 1), F32),
        jnp.zeros((tq, V_HEAD_DIM), F32),
    )
    carry = lax.fori_loop(0, n_full, step(False), init)
    _, l, acc = lax.fori_loop(n_full, n_all, step(True), carry)
    o_ref[...] = (acc / l).astype(BF16)


def _attention(q, kv, batch, sq, sk, sk_valid, q_pos0, tq, tk):
    nq = sq // tq
    v_col0 = MLA_HEADS * QK_PAD // V_HEAD_DIM
    return pl.pallas_call(
        functools.partial(_attn_kernel, tq=tq, tk=tk, sk_valid=sk_valid, q_pos0=q_pos0),
        grid=(batch, MLA_HEADS, nq),
        in_specs=[
            pl.BlockSpec((tq, QK_PAD), lambda b, h, i: (b * nq + i, h)),
            pl.BlockSpec((sk, QK_PAD), lambda b, h, i: (b, h)),
            pl.BlockSpec((sk, V_HEAD_DIM), lambda b, h, i: (b, v_col0 + h)),
        ],
        out_specs=pl.BlockSpec((tq, V_HEAD_DIM), lambda b, h, i: (b * nq + i, h)),
        out_shape=jax.ShapeDtypeStruct((batch * sq, MLA_WIDTH), BF16),
        compiler_params=_params("parallel", "parallel", "parallel"),
        name="attention",
    )(q, kv, kv)


def _gmlp_kernel(u_ref, v_ref, g_ref, b_ref, w_ref, bs_ref, o_ref, vn_ref):
    vn = _ln(v_ref[...], g_ref[...], b_ref[...])
    vn_ref[...] = vn
    bs = bs_ref[...]
    for g in range(GMLP_GROUPS):
        cols = slice(g * GMLP_GROUP_DIM, (g + 1) * GMLP_GROUP_DIM)
        mix = jnp.dot(w_ref[g], vn[:, cols].astype(BF16), preferred_element_type=F32)
        mix = mix + bs[:, g:g + 1]
        o_ref[:, cols] = (u_ref[:, cols].astype(F32) * mix).astype(BF16)


def _gmlp(zu, zv, g, b, w_sets, b_sets, n_prompt_chunks):
    t, n = zu.shape
    c = GMLP_CHUNK
    pick = lambda i: jnp.where(i >= n_prompt_chunks, 1, 0)
    return pl.pallas_call(
        _gmlp_kernel,
        grid=(t // c,),
        in_specs=[
            pl.BlockSpec((c, n), lambda i: (i, 0)),
            pl.BlockSpec((c, n), lambda i: (i, 0)),
            pl.BlockSpec((1, n), lambda i: (0, 0)),
            pl.BlockSpec((1, n), lambda i: (0, 0)),
            pl.BlockSpec((None, GMLP_GROUPS, c, c), lambda i: (pick(i), 0, 0, 0)),
            pl.BlockSpec((None, c, GMLP_GROUPS), lambda i: (pick(i), 0, 0)),
        ],
        out_specs=[
            pl.BlockSpec((c, n), lambda i: (i, 0)),
            pl.BlockSpec((c, n), lambda i: (i, 0)),
        ],
        out_shape=[
            jax.ShapeDtypeStruct((t, n), BF16),
            jax.ShapeDtypeStruct((t, n), F32),
        ],
        compiler_params=_params("parallel"),
        name="gmlp",
    )(zu, zv, g, b, w_sets, b_sets)


def _oproj_kernel(att_ref, g_ref, wa_ref, wg_ref, x_ref, lg_ref, lb_ref, o_ref, obf_ref, acc_ref, *, bn, alpha):
    j = pl.program_id(1)
    nj, bm, _ = acc_ref.shape
    acc_ref[j] = (
        jnp.dot(att_ref[...], wa_ref[...], preferred_element_type=F32)
        + jnp.dot(g_ref[...], wg_ref[...], preferred_element_type=F32)
    )

    @pl.when(j == nj - 1)
    def _():
        d = nj * bn
        s1 = jnp.zeros((bm, 1), F32)
        for c in range(nj):
            y = alpha * x_ref[:, c * bn:(c + 1) * bn] + acc_ref[c]
            acc_ref[c] = y
            s1 = s1 + jnp.sum(y, axis=-1, keepdims=True)
        mu = s1 / d
        s2 = jnp.zeros((bm, 1), F32)
        for c in range(nj):
            dev = acc_ref[c] - mu
            s2 = s2 + jnp.sum(dev * dev, axis=-1, keepdims=True)
        inv = lax.rsqrt(s2 / d + LN_EPS)
        for c in range(nj):
            cols = slice(c * bn, (c + 1) * bn)
            y = (acc_ref[c] - mu) * inv * lg_ref[:, cols] + lb_ref[:, cols]
            o_ref[:, cols] = y
            obf_ref[:, cols] = y.astype(BF16)


def _oproj(att, g, w_att, w_g, x, ln_g, ln_b, alpha, bm, bn):
    t, ka = att.shape
    kg = g.shape[1]
    d = x.shape[1]
    return pl.pallas_call(
        functools.partial(_oproj_kernel, bn=bn, alpha=alpha),
        grid=(t // bm, d // bn),
        in_specs=[
            pl.BlockSpec((bm, ka), lambda i, j: (i, 0)),
            pl.BlockSpec((bm, kg), lambda i, j: (i, 0)),
            pl.BlockSpec((ka, bn), lambda i, j: (0, j)),
            pl.BlockSpec((kg, bn), lambda i, j: (0, j)),
            pl.BlockSpec((bm, d), lambda i, j: (i, 0)),
            pl.BlockSpec((1, d), lambda i, j: (0, 0)),
            pl.BlockSpec((1, d), lambda i, j: (0, 0)),
        ],
        out_specs=[
            pl.BlockSpec((bm, d), lambda i, j: (i, 0)),
            pl.BlockSpec((bm, d), lambda i, j: (i, 0)),
        ],
        out_shape=[
            jax.ShapeDtypeStruct((t, d), F32),
            jax.ShapeDtypeStruct((t, d), BF16),
        ],
        scratch_shapes=[pltpu.VMEM((d // bn, bm, bn), F32)],
        compiler_params=_params("parallel", "arbitrary"),
        name="oproj_ln",
    )(att, g, w_att, w_g, x, ln_g, ln_b)


ROUTE_IDX, ROUTE_GATE, ROUTE_RANK = 0, TOP_K, 2 * TOP_K


def _router_kernel(x_ref, w_ref, b_ref, o_ref, cnt_out_ref, cnt_ref, *, bm):
    @pl.when(pl.program_id(0) == 0)
    def _():
        cnt_ref[...] = jnp.zeros_like(cnt_ref)

    logits = jnp.dot(x_ref[...], w_ref[...], preferred_element_type=F32,
                     precision=lax.Precision.HIGHEST) + b_ref[...]
    lane = lax.broadcasted_iota(jnp.int32, (bm, LANES), 1)
    work = logits
    vals, hits = [], []
    out = jnp.zeros((bm, LANES), F32)
    for k in range(TOP_K):
        m = jnp.max(work, axis=-1, keepdims=True)
        idx = jnp.min(jnp.where(work == m, lane, LANES), axis=-1, keepdims=True)
        hit = lane == idx
        work = jnp.where(hit, -jnp.inf, work)
        vals.append(m)
        hits.append(hit)
        out = jnp.where(lane == ROUTE_IDX + k, idx.astype(F32), out)
    exps = [jnp.exp(v - vals[0]) for v in vals]
    den = exps[0] + exps[1] + exps[2] + exps[3]
    member = jnp.zeros((bm, LANES), F32)
    for hit in hits:
        member = member + hit.astype(F32)
    row = lax.broadcasted_iota(jnp.int32, (bm, bm), 0)
    col = lax.broadcasted_iota(jnp.int32, (bm, bm), 1)
    before = (col < row).astype(BF16)
    prefix = jnp.dot(before, member.astype(BF16), preferred_element_type=F32) + cnt_ref[0:1, :]
    for k in range(TOP_K):
        out = jnp.where(lane == ROUTE_GATE + k, exps[k] / den, out)
        rank = jnp.sum(jnp.where(hits[k], prefix, 0.0), axis=-1, keepdims=True)
        out = jnp.where(lane == ROUTE_RANK + k, rank, out)
    o_ref[...] = out
    cnt_ref[0:1, :] = cnt_ref[0:1, :] + jnp.sum(member, axis=0, keepdims=True)
    cnt_out_ref[...] = cnt_ref[...]


def _router(x, w_pad, b_pad, bm):
    t, d = x.shape
    return pl.pallas_call(
        functools.partial(_router_kernel, bm=bm),
        grid=(t // bm,),
        in_specs=[
            pl.BlockSpec((bm, d), lambda i: (i, 0)),
            pl.BlockSpec((d, LANES), lambda i: (0, 0)),
            pl.BlockSpec((1, LANES), lambda i: (0, 0)),
        ],
        out_specs=[
            pl.BlockSpec((bm, LANES), lambda i: (i, 0)),
            pl.BlockSpec((8, LANES), lambda i: (0, 0)),
        ],
        out_shape=[
            jax.ShapeDtypeStruct((t, LANES), F32),
            jax.ShapeDtypeStruct((8, LANES), F32),
        ],
        scratch_shapes=[pltpu.VMEM((8, LANES), F32)],
        compiler_params=_params("arbitrary"),
        name="router",
    )(x, w_pad, b_pad)


def _dispatch_kernel(blk_ref, nact_ref, tok_ref, x_hbm, o_ref, buf, sem, *, rows):
    s = pl.program_id(0)
    n_active = nact_ref[0]

    def copy(step, slot, r):
        tok = tok_ref[blk_ref[step] * rows + r]
        return pltpu.make_async_copy(x_hbm.at[pl.ds(tok, 1)], buf.at[slot, pl.ds(r, 1)], sem.at[slot])

    def start(step, slot):
        def body(r, c):
            copy(step, slot, r).start()
            return c
        lax.fori_loop(0, rows, body, 0)

    def wait(step, slot):
        def body(r, c):
            copy(step, slot, r).wait()
            return c
        lax.fori_loop(0, rows, body, 0)

    @pl.when((s == 0) & (n_active > 0))
    def _():
        start(0, 0)

    @pl.when(s + 1 < n_active)
    def _():
        start(s + 1, (s + 1) % 2)

    @pl.when(s < n_active)
    def _():
        wait(s, s % 2)
        o_ref[...] = buf[s % 2].astype(BF16)


def _dispatch(x, slot_tok, act_blk, n_active, n_slots, rows):
    t, d = x.shape
    n_steps = act_blk.shape[0]
    return pl.pallas_call(
        functools.partial(_dispatch_kernel, rows=rows),
        grid_spec=pltpu.PrefetchScalarGridSpec(
            num_scalar_prefetch=3,
            grid=(n_steps,),
            in_specs=[pl.BlockSpec(memory_space=pl.ANY)],
            out_specs=pl.BlockSpec((rows, d), lambda s, blk, nact, tok: (blk[s], 0)),
            scratch_shapes=[pltpu.VMEM((2, rows, d), F32), pltpu.SemaphoreType.DMA((2,))],
        ),
        out_shape=jax.ShapeDtypeStruct((n_slots, d), BF16),
        compiler_params=_params("arbitrary"),
        name="dispatch",
    )(act_blk, n_active, slot_tok, x)


def _expert_glu_kernel(ue_ref, uw_ref, ur_ref, x_ref, wg_ref, wl_ref, bg_ref, bl_ref, o_ref):
    rows = ur_ref[pl.program_id(0)]

    @pl.when(rows > 0)
    def _():
        wg = wg_ref[...].astype(BF16)
        wl = wl_ref[...].astype(BF16)
        bg = bg_ref[...]
        bl = bl_ref[...]

        def body(s, c):
            r0 = pl.multiple_of(s * MOE_SUB, MOE_SUB)
            x = x_ref[pl.ds(r0, MOE_SUB), :]
            gate = jnp.dot(x, wg, preferred_element_type=F32) + bg
            lin = jnp.dot(x, wl, preferred_element_type=F32) + bl
            gate = jnp.minimum(gate, SWIGLU_LIMIT)
            lin = jnp.clip(lin, -SWIGLU_LIMIT, SWIGLU_LIMIT)
            act = gate * jax.nn.sigmoid(SWIGLU_ALPHA * gate) * (lin + 1.0)
            o_ref[pl.ds(r0, MOE_SUB), :] = act.astype(o_ref.dtype)
            return c

        lax.fori_loop(0, (rows + MOE_SUB - 1) // MOE_SUB, body, 0)


def _expert_down_kernel(ue_ref, uw_ref, ur_ref, x_ref, w_ref, b_ref, o_ref):
    rows = ur_ref[pl.program_id(0)]

    @pl.when(rows > 0)
    def _():
        w = w_ref[...].astype(BF16)
        b = b_ref[...]

        def body(s, c):
            r0 = pl.multiple_of(s * MOE_SUB, MOE_SUB)
            x = x_ref[pl.ds(r0, MOE_SUB), :]
            o_ref[pl.ds(r0, MOE_SUB), :] = jnp.dot(x, w, preferred_element_type=F32) + b
            return c

        lax.fori_loop(0, (rows + MOE_SUB - 1) // MOE_SUB, body, 0)


def _col(unit_rows, u, j, nt):
    return jnp.where(unit_rows[u] > 0, j, nt - 1)


def _out_block(unit_rows, u, j, n_units):
    used = unit_rows[u] > 0
    return jnp.where(used, u, n_units), jnp.where(used, j, 0)


def _expert_glu(x_sorted, w_gate_up, b_gate_up, unit_expert, unit_win, unit_rows):
    n_slots, d = x_sorted.shape
    n_e, _, two_h = w_gate_up.shape
    h = two_h // 2
    nt = h // MOE_TN
    n_units = unit_expert.shape[0]
    b3 = b_gate_up.reshape(n_e, 1, two_h)
    return pl.pallas_call(
        _expert_glu_kernel,
        grid_spec=pltpu.PrefetchScalarGridSpec(
            num_scalar_prefetch=3,
            grid=(n_units, nt),
            in_specs=[
                pl.BlockSpec((MOE_ROWS, d), lambda u, j, ue, uw, ur: (uw[u], 0)),
                pl.BlockSpec((None, d, MOE_TN), lambda u, j, ue, uw, ur: (ue[u], 0, _col(ur, u, j, nt))),
                pl.BlockSpec((None, d, MOE_TN), lambda u, j, ue, uw, ur: (ue[u], 0, nt + _col(ur, u, j, nt))),
                pl.BlockSpec((None, 1, MOE_TN), lambda u, j, ue, uw, ur: (ue[u], 0, _col(ur, u, j, nt))),
                pl.BlockSpec((None, 1, MOE_TN), lambda u, j, ue, uw, ur: (ue[u], 0, nt + _col(ur, u, j, nt))),
            ],
            out_specs=pl.BlockSpec((MOE_ROWS, MOE_TN), lambda u, j, ue, uw, ur: _out_block(ur, u, j, n_units)),
        ),
        out_shape=jax.ShapeDtypeStruct((n_slots + MOE_ROWS, h), BF16),
        compiler_params=_params("arbitrary", "arbitrary"),
        name="expert_glu",
    )(unit_expert, unit_win, unit_rows, x_sorted, w_gate_up, w_gate_up, b3, b3)


def _expert_down(act, w_down, b_down, unit_expert, unit_win, unit_rows):
    h = act.shape[1]
    n_e, _, d = w_down.shape
    n_units = unit_expert.shape[0]
    n_slots = (n_units + 1) * MOE_ROWS
    nt = d // MOE_TN
    b3 = b_down.reshape(n_e, 1, d)
    return pl.pallas_call(
        _expert_down_kernel,
        grid_spec=pltpu.PrefetchScalarGridSpec(
            num_scalar_prefetch=3,
            grid=(n_units, d // MOE_TN),
            in_specs=[
                pl.BlockSpec((MOE_ROWS, h), lambda u, j, ue, uw, ur: (uw[u], 0)),
                pl.BlockSpec((None, h, MOE_TN), lambda u, j, ue, uw, ur: (ue[u], 0, _col(ur, u, j, nt))),
                pl.BlockSpec((None, 1, MOE_TN), lambda u, j, ue, uw, ur: (ue[u], 0, _col(ur, u, j, nt))),
            ],
            out_specs=pl.BlockSpec((MOE_ROWS, MOE_TN), lambda u, j, ue, uw, ur: _out_block(ur, u, j, n_units)),
        ),
        out_shape=jax.ShapeDtypeStruct((n_slots, d), F32),
        compiler_params=_params("arbitrary", "arbitrary"),
        name="expert_down",
    )(unit_expert, unit_win, unit_rows, act, w_down, b3)


def _combine_kernel(dest_ref, y_hbm, x_ref, r_ref, lg_ref, lb_ref, o_ref, buf, sem, *, tb, alpha):
    s = pl.program_id(0)
    n_steps = pl.num_programs(0)

    def copy(step, slot, a):
        d = dest_ref[step * (tb * TOP_K) + a]
        return pltpu.make_async_copy(
            y_hbm.at[pl.ds(d, 1)], buf.at[slot, a % TOP_K, pl.ds(a // TOP_K, 1)], sem.at[slot])

    def start(step, slot):
        def body(a, c):
            copy(step, slot, a).start()
            return c
        lax.fori_loop(0, tb * TOP_K, body, 0)

    def wait(step, slot):
        def body(a, c):
            copy(step, slot, a).wait()
            return c
        lax.fori_loop(0, tb * TOP_K, body, 0)

    @pl.when(s == 0)
    def _():
        start(0, 0)

    @pl.when(s + 1 < n_steps)
    def _():
        start(s + 1, (s + 1) % 2)

    wait(s, s % 2)
    route = r_ref[...]
    f = jnp.zeros(x_ref.shape, F32)
    for k in range(TOP_K):
        f = f + buf[s % 2, k] * route[:, ROUTE_GATE + k:ROUTE_GATE + k + 1]
    o_ref[...] = _ln(alpha * x_ref[...] + f, lg_ref[...], lb_ref[...])


def _combine(dest_flat, y_slots, x, route, ln_g, ln_b, alpha, tb):
    t, d = x.shape
    return pl.pallas_call(
        functools.partial(_combine_kernel, tb=tb, alpha=alpha),
        grid_spec=pltpu.PrefetchScalarGridSpec(
            num_scalar_prefetch=1,
            grid=(t // tb,),
            in_specs=[
                pl.BlockSpec(memory_space=pl.ANY),
                pl.BlockSpec((tb, d), lambda s, dest: (s, 0)),
                pl.BlockSpec((tb, LANES), lambda s, dest: (s, 0)),
                pl.BlockSpec((1, d), lambda s, dest: (0, 0)),
                pl.BlockSpec((1, d), lambda s, dest: (0, 0)),
            ],
            out_specs=pl.BlockSpec((tb, d), lambda s, dest: (s, 0)),
            scratch_shapes=[pltpu.VMEM((2, TOP_K, tb, d), F32), pltpu.SemaphoreType.DMA((2,))],
        ),
        out_shape=jax.ShapeDtypeStruct((t, d), F32),
        compiler_params=_params("arbitrary"),
        name="combine_ln",
    )(dest_flat, y_slots, x, route, ln_g, ln_b)


def _rot_cols(w):
    half = ROPE_DIM // 2
    return jnp.concatenate([-w[..., half:], w[..., :half]], axis=-1)


def _rope_cs(pos):
    half = ROPE_DIM // 2
    inv_freq = ROPE_THETA ** (-jnp.arange(half, dtype=F32) / half)
    ang = pos.astype(F32)[:, None] * inv_freq[None, :]
    cos, sin = jnp.cos(ang), jnp.sin(ang)
    return jnp.concatenate([cos, cos, sin, sin], axis=-1)


def _kv_weight(w_uk, w_uv):
    wk = jnp.zeros((KV_IN, MLA_HEADS, QK_PAD), F32)
    wk = wk.at[:KV_LORA_RANK, :, :QK_NOPE_DIM].set(w_uk)
    eye = jnp.broadcast_to(jnp.eye(ROPE_DIM, dtype=F32)[:, None, :], (ROPE_DIM, MLA_HEADS, ROPE_DIM))
    wk = wk.at[KV_LORA_RANK:KV_LORA_RANK + ROPE_DIM, :, QK_NOPE_DIM:QK_NOPE_DIM + ROPE_DIM].set(eye)
    wv = jnp.zeros((KV_IN, MLA_HEADS, V_HEAD_DIM), F32).at[:KV_LORA_RANK].set(w_uv)
    return jnp.concatenate(
        [wk.reshape(KV_IN, MLA_HEADS * QK_PAD), wv.reshape(KV_IN, MLA_WIDTH)], axis=1).astype(BF16)


def _routing_tables(route, counts, n_tok):
    n_assign = n_tok * TOP_K
    n_units = n_assign // MOE_ROWS + N_EXPERTS
    n_slots = n_units * MOE_ROWS
    e = route[:, ROUTE_IDX:ROUTE_IDX + TOP_K].astype(jnp.int32)
    rank = route[:, ROUTE_RANK:ROUTE_RANK + TOP_K].astype(jnp.int32)
    units_per_e = (counts + MOE_ROWS - 1) // MOE_ROWS
    unit_end = jnp.cumsum(units_per_e)
    unit_start = unit_end - units_per_e
    total_units = unit_end[-1]
    dest = (unit_start[e] * MOE_ROWS + rank).reshape(-1)
    u = jnp.arange(n_units, dtype=jnp.int32)
    u_eff = jnp.minimum(u, total_units - 1)
    unit_expert = jnp.minimum(jnp.searchsorted(unit_end, u_eff, side="right"), N_EXPERTS - 1).astype(jnp.int32)
    rows = jnp.clip(counts[unit_expert] - (u_eff - unit_start[unit_expert]) * MOE_ROWS, 0, MOE_ROWS)
    unit_rows = jnp.where(u < total_units, rows, 0).astype(jnp.int32)
    unit_win = u_eff.astype(jnp.int32)
    slot_tok = jnp.zeros((n_slots,), jnp.int32).at[dest].set(jnp.arange(n_assign, dtype=jnp.int32) // TOP_K)
    sub_per_unit = MOE_ROWS // MOE_SUB
    n_steps = n_assign // MOE_SUB + N_EXPERTS
    blk_per_unit = (unit_rows + MOE_SUB - 1) // MOE_SUB
    blk_end = jnp.cumsum(blk_per_unit)
    n_active = blk_end[-1]
    st = jnp.minimum(jnp.arange(n_steps, dtype=jnp.int32), n_active - 1)
    st_unit = jnp.minimum(jnp.searchsorted(blk_end, st, side="right"), n_units - 1).astype(jnp.int32)
    act_blk = (st_unit * sub_per_unit + st - (blk_end - blk_per_unit)[st_unit]).astype(jnp.int32)
    return dest.astype(jnp.int32), slot_tok, act_blk, n_active.reshape(1).astype(jnp.int32), \
        unit_expert, unit_win, unit_rows, n_slots


def _pick_block(n, target, mult):
    best = None
    for b in range(mult, min(n, target) + 1, mult):
        if n % b == 0:
            best = b
    assert best is not None, (n, target, mult)
    return best


def kernel(x_prompt, x_sample, cache_kv_latent, cache_k_rope, w_in, q_a_norm_g, kv_a_norm_g, w_q_b, w_uk, w_uv, gmlp_norm_g, gmlp_norm_b, gmlp_w_s, gmlp_b_s, w_o, ln1_g, ln1_b, w_router, b_router, w_gate_up, b_gate_up, w_down, b_down, ln2_g, ln2_b):
    batch, seq, d_model = x_prompt.shape
    dbatch, dseq, _ = x_sample.shape
    depth, _, past, _ = cache_kv_latent.shape
    n_p, n_s = batch * seq, dbatch * dseq
    n_tok = n_p + n_s
    alpha = (2 * depth) ** 0.25
    assert n_p % GMLP_CHUNK == 0 and n_s % GMLP_CHUNK == 0 and GMLP_CHUNK % dseq == 0

    bm_big = _pick_block(n_tok, 1088, 16)
    bm_mid = _pick_block(n_tok, 512, 16)
    bm_small = _pick_block(n_tok, 256, 16)

    pos = jnp.concatenate([
        jnp.tile(jnp.arange(seq, dtype=jnp.int32), batch),
        jnp.tile(past + jnp.arange(dseq, dtype=jnp.int32), dbatch)])
    kcs = _rope_cs(pos)
    qtab = jnp.concatenate([jnp.ones((n_tok, QK_NOPE_DIM), F32), kcs], axis=-1) * ATTN_SCALE

    x_p = x_prompt.reshape(n_p, d_model)
    x_s = x_sample.reshape(n_s, d_model)
    x = jnp.concatenate([x_p, x_s], axis=0)

    lat_p, kr_p_all, lat_s, kr_s_all, gv_s_all = [], [], [], [], []
    for l in range(depth):
        w_kr = w_in[l][:, COL_KV:COL_KR]
        w_lat = jnp.concatenate([w_in[l][:, :COL_KR], _rot_cols(w_kr)], axis=1).astype(BF16)
        w_zu = w_in[l][:, COL_KR:COL_KR + GMLP_WIDTH].astype(BF16)
        w_zv = w_in[l][:, COL_KR + GMLP_WIDTH:].astype(BF16)
        wq = w_q_b[l]
        w_q = jnp.concatenate(
            [wq, _rot_cols(wq[..., QK_NOPE_DIM:])], axis=-1).reshape(Q_LORA_RANK, MLA_HEADS * QK_PAD).astype(BF16)
        w_kv = _kv_weight(w_uk[l], w_uv[l])
        cpos = jnp.arange(GMLP_CHUNK)
        cmask = (cpos[None, :] // CHUNK) <= (cpos[:, None] // CHUNK)
        w_sp_p = jnp.where(cmask[None], gmlp_w_s[l], 0.0)
        reps = GMLP_CHUNK // dseq
        dpos = jnp.arange(dseq)
        dmask = (dpos[None, :] // CHUNK) <= (dpos[:, None] // CHUNK)
        w_d = jnp.where(dmask[None], gmlp_w_s[l][:, :dseq, :dseq], 0.0)
        w_sp_s = jnp.einsum("ab,gij->gaibj", jnp.eye(reps, dtype=F32), w_d).reshape(
            GMLP_GROUPS, GMLP_CHUNK, GMLP_CHUNK)
        w_sets = jnp.stack([w_sp_p, w_sp_s]).astype(BF16)
        b_sets = jnp.stack([gmlp_b_s[l].T, jnp.tile(gmlp_b_s[l][:, :dseq], (1, reps)).T])
        w_o_bf = w_o[l].astype(BF16)
        w_r = jnp.zeros((d_model, LANES), F32).at[:, :N_EXPERTS].set(w_router[l])
        b_r = jnp.full((1, LANES), -jnp.inf, F32).at[0, :N_EXPERTS].set(b_router[l])

        x_bf = x.astype(BF16)
        cq, ckv, kpe, a_new = _inproj_lat(
            x_bf, w_lat, q_a_norm_g[l][None], kv_a_norm_g[l][None], kcs, bm_mid)
        zu = _inproj_gelu(x_bf, w_zu, BF16, bm_big, 512)
        zv = _inproj_gelu(x_bf, w_zv, F32, bm_big, 512)
        q = _qproj(cq, w_q, qtab, bm_big)

        kv_p = _matmul(a_new[:n_p], w_kv, BF16, _pick_block(n_p, 1024, 16), 1024, "kv_prompt")
        tq = min(512, seq)
        att_p = _attention(q[:n_p], kv_p, batch, seq, seq, seq, 0, tq, tq)

        sk_valid = past + dseq
        tk_s = 384
        sk = (sk_valid + tk_s - 1) // tk_s * tk_s
        cache_a = jnp.concatenate([
            cache_kv_latent[l], cache_k_rope[l],
            jnp.zeros((dbatch, past, KV_IN - KV_LORA_RANK - ROPE_DIM), F32)], axis=-1).astype(BF16)
        a_s = jnp.concatenate([
            cache_a, a_new[n_p:].reshape(dbatch, dseq, KV_IN),
            jnp.zeros((dbatch, sk - sk_valid, KV_IN), BF16)], axis=1).reshape(dbatch * sk, KV_IN)
        kv_s = _matmul(a_s, w_kv, BF16, sk, 1024, "kv_sample")
        att_s = _attention(q[n_p:], kv_s, dbatch, dseq, sk, sk_valid, past, dseq, tk_s)
        att = jnp.concatenate([att_p, att_s], axis=0)

        g, vn = _gmlp(zu, zv, gmlp_norm_g[l][None], gmlp_norm_b[l][None], w_sets, b_sets, n_p // GMLP_CHUNK)
        x1, x1_bf = _oproj(att, g, w_o_bf[:MLA_WIDTH], w_o_bf[MLA_WIDTH:], x, ln1_g[l][None], ln1_b[l][None],
                           alpha, bm_small, 512)

        route, cnt = _router(x1, w_r, b_r, bm_mid)
        counts = cnt[0, :N_EXPERTS].astype(jnp.int32)
        dest, slot_tok, act_blk, n_active, unit_expert, unit_win, unit_rows, n_slots = _routing_tables(
            route, counts, n_tok)
        x_sorted = _dispatch(x1, slot_tok, act_blk, n_active, n_slots, MOE_SUB)
        act = _expert_glu(x_sorted, w_gate_up[l], b_gate_up[l], unit_expert, unit_win, unit_rows)
        y_slots = _expert_down(act, w_down[l], b_down[l], unit_expert, unit_win, unit_rows)
        x = _combine(dest, y_slots, x1, route, ln2_g[l][None], ln2_b[l][None], alpha, 64)

        lat_p.append(ckv[:n_p].reshape(batch, seq, KV_LORA_RANK))
        kr_p_all.append(kpe[:n_p].reshape(batch, seq, ROPE_DIM))
        lat_s.append(ckv[n_p:].reshape(dbatch, dseq, KV_LORA_RANK))
        kr_s_all.append(kpe[n_p:].reshape(dbatch, dseq, ROPE_DIM))
        gv_s_all.append(vn[n_p:].reshape(dbatch, dseq, GMLP_GROUPS, GMLP_GROUP_DIM))

    return (
        x[:n_p].reshape(batch, seq, d_model),
        x[n_p:].reshape(dbatch, dseq, d_model),
        jnp.stack(lat_p), jnp.stack(kr_p_all), jnp.stack(lat_s), jnp.stack(kr_s_all), jnp.stack(gv_s_all),
    )
```

```python
import functools
import math

import jax
import jax.numpy as jnp
from jax import lax
from jax.experimental import pallas as pl
from jax.experimental.pallas import tpu as pltpu

F32 = jnp.float32
BF16 = jnp.bfloat16

CHUNK = 64
MLA_HEADS = 16
QK_NOPE_DIM = 128
ROPE_DIM = 64
V_HEAD_DIM = 128
Q_LORA_RANK = 768
KV_LORA_RANK = 512
ROPE_THETA = 10000.0
ATTN_SCALE = (QK_NOPE_DIM + ROPE_DIM) ** -0.5
GMLP_GROUPS = 16
GMLP_GROUP_DIM = 128
GMLP_WIDTH = GMLP_GROUPS * GMLP_GROUP_DIM
GMLP_CHUNK = 128
MLA_WIDTH = MLA_HEADS * V_HEAD_DIM
COL_Q = Q_LORA_RANK
COL_KV = COL_Q + KV_LORA_RANK
COL_KR = COL_KV + ROPE_DIM
N_EXPERTS = 32
TOP_K = 4
SWIGLU_ALPHA = 1.702
SWIGLU_LIMIT = 7.0
LN_EPS = 1e-5
RMS_EPS = 1e-6

LANES = 128
QK_PAD = 256
KV_IN = KV_LORA_RANK + LANES
KV_OUT = MLA_HEADS * QK_PAD + MLA_HEADS * V_HEAD_DIM
MOE_ROWS = 1280
MOE_SUB = 256
MOE_BIG = 512
MOE_ALIGN = 128
MOE_TN = 256
MOE_TN_DOWN = 512
DMA_UNROLL = 8
VMEM_LIMIT = 56 * 1024 * 1024


def _params(*sem):
    return pltpu.CompilerParams(dimension_semantics=sem, vmem_limit_bytes=VMEM_LIMIT)


def _rms(x, g):
    return x * lax.rsqrt(jnp.mean(x * x, axis=-1, keepdims=True) + RMS_EPS) * g


def _ln(x, g, b):
    mu = jnp.mean(x, axis=-1, keepdims=True)
    d = x - mu
    var = jnp.mean(d * d, axis=-1, keepdims=True)
    return d * lax.rsqrt(var + LN_EPS) * g + b


def _inproj_lat_kernel(xp_ref, xs_ref, w_ref, gq_ref, gkv_ref, cs_ref, cq_ref, ckv_ref, kpe_ref, a_ref, xbf_ref,
                       *, n_prompt_blocks):
    i = pl.program_id(0)

    @pl.when(i < n_prompt_blocks)
    def _():
        xbf_ref[...] = xp_ref[...].astype(BF16)

    @pl.when(i >= n_prompt_blocks)
    def _():
        xbf_ref[...] = xs_ref[...].astype(BF16)

    p = jnp.dot(xbf_ref[...], w_ref[...], preferred_element_type=F32)
    cq_ref[...] = _rms(p[:, :COL_Q], gq_ref[...]).astype(BF16)
    ckv = _rms(p[:, COL_Q:COL_KV], gkv_ref[...])
    ckv_ref[...] = ckv
    t = p[:, COL_KV:] * cs_ref[...]
    r = t + pltpu.roll(t, ROPE_DIM, axis=1)
    kpe_ref[...] = r[:, :ROPE_DIM]
    a_ref[:, :KV_LORA_RANK] = ckv.astype(BF16)
    a_ref[:, KV_LORA_RANK:] = r.astype(BF16)


def _split_rows(n_prompt_blocks):
    prompt = lambda i: jnp.minimum(i, n_prompt_blocks - 1)
    sample = lambda i: jnp.maximum(i - n_prompt_blocks, 0)
    return prompt, sample


def _inproj_lat(x_p, x_s, w_lat, gq, gkv, kcs, bm):
    (n_p, d), n_s = x_p.shape, x_s.shape[0]
    t = n_p + n_s
    n = w_lat.shape[1]
    npb = n_p // bm
    prow, srow = _split_rows(npb)
    return pl.pallas_call(
        functools.partial(_inproj_lat_kernel, n_prompt_blocks=npb),
        grid=(t // bm,),
        in_specs=[
            pl.BlockSpec((bm, d), lambda i: (prow(i), 0)),
            pl.BlockSpec((bm, d), lambda i: (srow(i), 0)),
            pl.BlockSpec((d, n), lambda i: (0, 0)),
            pl.BlockSpec((1, COL_Q), lambda i: (0, 0)),
            pl.BlockSpec((1, KV_LORA_RANK), lambda i: (0, 0)),
            pl.BlockSpec((bm, LANES), lambda i: (i, 0)),
        ],
        out_specs=[
            pl.BlockSpec((bm, COL_Q), lambda i: (i, 0)),
            pl.BlockSpec((bm, KV_LORA_RANK), lambda i: (i, 0)),
            pl.BlockSpec((bm, ROPE_DIM), lambda i: (i, 0)),
            pl.BlockSpec((bm, KV_IN), lambda i: (i, 0)),
            pl.BlockSpec((bm, d), lambda i: (i, 0)),
        ],
        out_shape=[
            jax.ShapeDtypeStruct((t, COL_Q), BF16),
            jax.ShapeDtypeStruct((t, KV_LORA_RANK), F32),
            jax.ShapeDtypeStruct((t, ROPE_DIM), F32),
            jax.ShapeDtypeStruct((t, KV_IN), BF16),
            jax.ShapeDtypeStruct((t, d), BF16),
        ],
        compiler_params=_params("parallel"),
        name="inproj_lat",
    )(x_p, x_s, w_lat, gq, gkv, kcs)


def _inproj_gelu_kernel(x_ref, w_ref, o_ref):
    p = jnp.dot(x_ref[...], w_ref[...], preferred_element_type=F32)
    o_ref[...] = (0.5 * p * (1.0 + lax.erf(p * (2.0 ** -0.5)))).astype(o_ref.dtype)


def _inproj_gelu(x_bf, w, out_dtype, bm, bn):
    t, d = x_bf.shape
    n = w.shape[1]
    return pl.pallas_call(
        _inproj_gelu_kernel,
        grid=(t // bm, n // bn),
        in_specs=[
            pl.BlockSpec((bm, d), lambda i, j: (i, 0)),
            pl.BlockSpec((d, bn), lambda i, j: (0, j)),
        ],
        out_specs=pl.BlockSpec((bm, bn), lambda i, j: (i, j)),
        out_shape=jax.ShapeDtypeStruct((t, n), out_dtype),
        compiler_params=_params("parallel", "parallel"),
        name="inproj_gelu",
    )(x_bf, w)


def _qproj_kernel(c_ref, w_ref, tab_ref, o_ref):
    p = jnp.dot(c_ref[...], w_ref[...], preferred_element_type=F32) * tab_ref[...]
    o_ref[:, :QK_NOPE_DIM] = p[:, :QK_NOPE_DIM].astype(BF16)
    t = p[:, QK_NOPE_DIM:]
    o_ref[:, QK_NOPE_DIM:] = (t + pltpu.roll(t, ROPE_DIM, axis=1)).astype(BF16)


def _qproj(cq, w_q, qtab, bm):
    t, r = cq.shape
    return pl.pallas_call(
        _qproj_kernel,
        grid=(t // bm, MLA_HEADS),
        in_specs=[
            pl.BlockSpec((bm, r), lambda i, h: (i, 0)),
            pl.BlockSpec((r, QK_PAD), lambda i, h: (0, h)),
            pl.BlockSpec((bm, QK_PAD), lambda i, h: (i, 0)),
        ],
        out_specs=pl.BlockSpec((bm, QK_PAD), lambda i, h: (i, h)),
        out_shape=jax.ShapeDtypeStruct((t, MLA_HEADS * QK_PAD), BF16),
        compiler_params=_params("parallel", "parallel"),
        name="qproj",
    )(cq, w_q, qtab)


def _matmul_kernel(x_ref, w_ref, o_ref):
    o_ref[...] = jnp.dot(x_ref[...], w_ref[...], preferred_element_type=F32).astype(o_ref.dtype)


def _matmul(x, w, out_dtype, bm, bn, name):
    m, k = x.shape
    n = w.shape[1]
    return pl.pallas_call(
        _matmul_kernel,
        grid=(m // bm, n // bn),
        in_specs=[
            pl.BlockSpec((bm, k), lambda i, j: (i, 0)),
            pl.BlockSpec((k, bn), lambda i, j: (0, j)),
        ],
        out_specs=pl.BlockSpec((bm, bn), lambda i, j: (i, j)),
        out_shape=jax.ShapeDtypeStruct((m, n), out_dtype),
        compiler_params=_params("parallel", "parallel"),
        name=name,
    )(x, w)


def _attn_kernel(q_ref, k_ref, v_ref, o_ref, *, tq, tk, sk_valid, q_pos0):
    i = pl.program_id(2)
    q = q_ref[...]
    q_lo = q_pos0 + i * tq
    seen_by_all = jnp.minimum((q_lo // CHUNK + 1) * CHUNK, sk_valid)
    seen_by_any = jnp.minimum(((q_lo + tq - 1) // CHUNK + 1) * CHUNK, sk_valid)
    n_full = seen_by_all // tk
    n_all = (seen_by_any + tk - 1) // tk

    def step(masked):
        def body(j, carry):
            m, l, acc = carry
            k0 = pl.multiple_of(j * tk, tk)
            k = k_ref[pl.ds(k0, tk), :]
            s = lax.dot_general(q, k, (((1,), (1,)), ((), ())), preferred_element_type=F32)
            if masked:
                qpos = q_lo + lax.broadcasted_iota(jnp.int32, (tq, tk), 0)
                kpos = k0 + lax.broadcasted_iota(jnp.int32, (tq, tk), 1)
                ok = (kpos // CHUNK <= qpos // CHUNK) & (kpos < sk_valid)
                s = jnp.where(ok, s, -jnp.inf)
            m_new = jnp.maximum(m, jnp.max(s, axis=-1, keepdims=True))
            a = jnp.exp(m - m_new)
            p = jnp.exp(s - m_new)
            l = a * l + jnp.sum(p, axis=-1, keepdims=True)
            v = v_ref[pl.ds(k0, tk), :]
            acc = a * acc + jnp.dot(p.astype(BF16), v, preferred_element_type=F32)
            return m_new, l, acc

        return body

    init = (
        jnp.full((tq, 1), -1e30, F32),
        jnp.zeros((tq, 1), F32),
        jnp.zeros((tq, V_HEAD_DIM), F32),
    )
    carry = lax.fori_loop(0, n_full, step(False), init)
    _, l, acc = lax.fori_loop(n_full, n_all, step(True), carry)
    o_ref[...] = (acc / l).astype(BF16)


def _attention(q, q_row0, kv, batch, sq, sk, sk_valid, q_pos0, tq, tk):
    nq = sq // tq
    qb0 = q_row0 // tq
    v_col0 = MLA_HEADS * QK_PAD // V_HEAD_DIM
    return pl.pallas_call(
        functools.partial(_attn_kernel, tq=tq, tk=tk, sk_valid=sk_valid, q_pos0=q_pos0),
        grid=(batch, MLA_HEADS, nq),
        in_specs=[
            pl.BlockSpec((tq, QK_PAD), lambda b, h, i: (qb0 + b * nq + i, h)),
            pl.BlockSpec((sk, QK_PAD), lambda b, h, i: (b, h)),
            pl.BlockSpec((sk, V_HEAD_DIM), lambda b, h, i: (b, v_col0 + h)),
        ],
        out_specs=pl.BlockSpec((tq, V_HEAD_DIM), lambda b, h, i: (b * nq + i, h)),
        out_shape=jax.ShapeDtypeStruct((batch * sq, MLA_WIDTH), BF16),
        compiler_params=_params("parallel", "parallel", "parallel"),
        name="attention",
    )(q, kv, kv)


def _gmlp_kernel(u_ref, v_ref, g_ref, b_ref, w_ref, bs_ref, o_ref, vn_ref):
    vn = _ln(v_ref[...], g_ref[...], b_ref[...])
    vn_ref[...] = vn
    bs = bs_ref[...]
    for g in range(GMLP_GROUPS):
        cols = slice(g * GMLP_GROUP_DIM, (g + 1) * GMLP_GROUP_DIM)
        mix = jnp.dot(w_ref[g], vn[:, cols].astype(BF16), preferred_element_type=F32)
        mix = mix + bs[:, g:g + 1]
        o_ref[:, cols] = (u_ref[:, cols].astype(F32) * mix).astype(BF16)


def _gmlp(zu, zv, g, b, w_sets, b_sets, n_prompt_chunks):
    t, n = zu.shape
    c = GMLP_CHUNK
    pick = lambda i: jnp.where(i >= n_prompt_chunks, 1, 0)
    return pl.pallas_call(
        _gmlp_kernel,
        grid=(t // c,),
        in_specs=[
            pl.BlockSpec((c, n), lambda i: (i, 0)),
            pl.BlockSpec((c, n), lambda i: (i, 0)),
            pl.BlockSpec((1, n), lambda i: (0, 0)),
            pl.BlockSpec((1, n), lambda i: (0, 0)),
            pl.BlockSpec((None, GMLP_GROUPS, c, c), lambda i: (pick(i), 0, 0, 0)),
            pl.BlockSpec((None, c, GMLP_GROUPS), lambda i: (pick(i), 0, 0)),
        ],
        out_specs=[
            pl.BlockSpec((c, n), lambda i: (i, 0)),
            pl.BlockSpec((c, n), lambda i: (i, 0)),
        ],
        out_shape=[
            jax.ShapeDtypeStruct((t, n), BF16),
            jax.ShapeDtypeStruct((t, n), F32),
        ],
        compiler_params=_params("parallel"),
        name="gmlp",
    )(zu, zv, g, b, w_sets, b_sets)


def _oproj_kernel(att_ref, g_ref, wa_ref, wg_ref, xp_ref, xs_ref, lg_ref, lb_ref, o_ref, acc_ref,
                  *, bn, alpha, n_prompt_blocks):
    i = pl.program_id(0)
    j = pl.program_id(1)
    nj, bm, _ = acc_ref.shape
    acc_ref[j] = (
        jnp.dot(att_ref[...], wa_ref[...], preferred_element_type=F32)
        + jnp.dot(g_ref[...], wg_ref[...], preferred_element_type=F32)
    )

    def finish(x_ref):
        d = nj * bn
        s1 = jnp.zeros((bm, 1), F32)
        for c in range(nj):
            y = alpha * x_ref[:, c * bn:(c + 1) * bn] + acc_ref[c]
            acc_ref[c] = y
            s1 = s1 + jnp.sum(y, axis=-1, keepdims=True)
        mu = s1 / d
        s2 = jnp.zeros((bm, 1), F32)
        for c in range(nj):
            dev = acc_ref[c] - mu
            s2 = s2 + jnp.sum(dev * dev, axis=-1, keepdims=True)
        inv = lax.rsqrt(s2 / d + LN_EPS)
        for c in range(nj):
            cols = slice(c * bn, (c + 1) * bn)
            o_ref[:, cols] = (acc_ref[c] - mu) * inv * lg_ref[:, cols] + lb_ref[:, cols]

    @pl.when((j == nj - 1) & (i < n_prompt_blocks))
    def _():
        finish(xp_ref)

    @pl.when((j == nj - 1) & (i >= n_prompt_blocks))
    def _():
        finish(xs_ref)


def _oproj(att, g, w_att, w_g, x_p, x_s, ln_g, ln_b, alpha, bm, bn):
    t, ka = att.shape
    kg = g.shape[1]
    n_p, d = x_p.shape
    npb = n_p // bm
    prow, srow = _split_rows(npb)
    return pl.pallas_call(
        functools.partial(_oproj_kernel, bn=bn, alpha=alpha, n_prompt_blocks=npb),
        grid=(t // bm, d // bn),
        in_specs=[
            pl.BlockSpec((bm, ka), lambda i, j: (i, 0)),
            pl.BlockSpec((bm, kg), lambda i, j: (i, 0)),
            pl.BlockSpec((ka, bn), lambda i, j: (0, j)),
            pl.BlockSpec((kg, bn), lambda i, j: (0, j)),
            pl.BlockSpec((bm, d), lambda i, j: (prow(i), 0)),
            pl.BlockSpec((bm, d), lambda i, j: (srow(i), 0)),
            pl.BlockSpec((1, d), lambda i, j: (0, 0)),
            pl.BlockSpec((1, d), lambda i, j: (0, 0)),
        ],
        out_specs=pl.BlockSpec((bm, d), lambda i, j: (i, 0)),
        out_shape=jax.ShapeDtypeStruct((t, d), F32),
        scratch_shapes=[pltpu.VMEM((d // bn, bm, bn), F32)],
        compiler_params=_params("parallel", "arbitrary"),
        name="oproj_ln",
    )(att, g, w_att, w_g, x_p, x_s, ln_g, ln_b)


ROUTE_IDX, ROUTE_GATE, ROUTE_RANK = 0, TOP_K, 2 * TOP_K


def _router_kernel(x_ref, w_ref, b_ref, o_ref, cnt_out_ref, cnt_ref, *, bm):
    @pl.when(pl.program_id(0) == 0)
    def _():
        cnt_ref[...] = jnp.zeros_like(cnt_ref)

    logits = jnp.dot(x_ref[...], w_ref[...], preferred_element_type=F32,
                     precision=lax.Precision.HIGHEST) + b_ref[...]
    lane = lax.broadcasted_iota(jnp.int32, (bm, LANES), 1)
    work = logits
    vals, hits = [], []
    out = jnp.zeros((bm, LANES), F32)
    for k in range(TOP_K):
        m = jnp.max(work, axis=-1, keepdims=True)
        idx = jnp.min(jnp.where(work == m, lane, LANES), axis=-1, keepdims=True)
        hit = lane == idx
        work = jnp.where(hit, -jnp.inf, work)
        vals.append(m)
        hits.append(hit)
        out = jnp.where(lane == ROUTE_IDX + k, idx.astype(F32), out)
    exps = [jnp.exp(v - vals[0]) for v in vals]
    den = exps[0] + exps[1] + exps[2] + exps[3]
    member = jnp.zeros((bm, LANES), F32)
    for hit in hits:
        member = member + hit.astype(F32)
    row = lax.broadcasted_iota(jnp.int32, (bm, bm), 0)
    col = lax.broadcasted_iota(jnp.int32, (bm, bm), 1)
    before = (col < row).astype(BF16)
    prefix = jnp.dot(before, member.astype(BF16), preferred_element_type=F32) + cnt_ref[0:1, :]
    for k in range(TOP_K):
        out = jnp.where(lane == ROUTE_GATE + k, exps[k] / den, out)
        rank = jnp.sum(jnp.where(hits[k], prefix, 0.0), axis=-1, keepdims=True)
        out = jnp.where(lane == ROUTE_RANK + k, rank, out)
    o_ref[...] = out
    cnt_ref[0:1, :] = cnt_ref[0:1, :] + jnp.sum(member, axis=0, keepdims=True)
    cnt_out_ref[...] = cnt_ref[...]


def _router(x, w_pad, b_pad, bm):
    t, d = x.shape
    return pl.pallas_call(
        functools.partial(_router_kernel, bm=bm),
        grid=(t // bm,),
        in_specs=[
            pl.BlockSpec((bm, d), lambda i: (i, 0)),
            pl.BlockSpec((d, LANES), lambda i: (0, 0)),
            pl.BlockSpec((1, LANES), lambda i: (0, 0)),
        ],
        out_specs=[
            pl.BlockSpec((bm, LANES), lambda i: (i, 0)),
            pl.BlockSpec((8, LANES), lambda i: (0, 0)),
        ],
        out_shape=[
            jax.ShapeDtypeStruct((t, LANES), F32),
            jax.ShapeDtypeStruct((8, LANES), F32),
        ],
        scratch_shapes=[pltpu.VMEM((8, LANES), F32)],
        compiler_params=_params("arbitrary"),
        name="router",
    )(x, w_pad, b_pad)


def _dispatch_kernel(blk_ref, nact_ref, tok_ref, x_hbm, o_ref, buf, sem, *, rows):
    s = pl.program_id(0)
    n_active = nact_ref[0]

    def start(step, slot):
        base = blk_ref[step] * rows

        def body(r, c):
            tok = tok_ref[base + r]
            pltpu.make_async_copy(x_hbm.at[pl.ds(tok, 1)], buf.at[slot, pl.ds(r, 1)], sem.at[slot]).start()
            return c
        lax.fori_loop(0, rows, body, 0, unroll=DMA_UNROLL)

    def wait(step, slot):
        pltpu.make_async_copy(x_hbm.at[pl.ds(0, rows)], buf.at[slot], sem.at[slot]).wait()

    @pl.when((s == 0) & (n_active > 0))
    def _():
        start(0, 0)

    @pl.when(s + 1 < n_active)
    def _():
        start(s + 1, (s + 1) % 2)

    @pl.when(s < n_active)
    def _():
        wait(s, s % 2)
        o_ref[...] = buf[s % 2].astype(BF16)


def _dispatch(x, slot_tok, act_blk, n_active, n_slots, rows):
    t, d = x.shape
    n_steps = act_blk.shape[0]
    return pl.pallas_call(
        functools.partial(_dispatch_kernel, rows=rows),
        grid_spec=pltpu.PrefetchScalarGridSpec(
            num_scalar_prefetch=3,
            grid=(n_steps,),
            in_specs=[pl.BlockSpec(memory_space=pl.ANY)],
            out_specs=pl.BlockSpec((rows, d), lambda s, blk, nact, tok: (blk[s], 0)),
            scratch_shapes=[pltpu.VMEM((2, rows, d), F32), pltpu.SemaphoreType.DMA((2,))],
        ),
        out_shape=jax.ShapeDtypeStruct((n_slots, d), BF16),
        compiler_params=_params("arbitrary"),
        name="dispatch",
    )(act_blk, n_active, slot_tok, x)


def _for_row_blocks(rows, fn):
    padded = (rows + MOE_ALIGN - 1) // MOE_ALIGN * MOE_ALIGN
    n_big = padded // MOE_BIG

    def body(s, c):
        fn(pl.multiple_of(s * MOE_BIG, MOE_BIG), MOE_BIG)
        return c

    lax.fori_loop(0, n_big, body, 0)
    base = n_big * MOE_BIG
    rem = padded - base

    @pl.when(rem >= 2 * MOE_ALIGN)
    def _():
        fn(pl.multiple_of(base, 2 * MOE_ALIGN), 2 * MOE_ALIGN)

    @pl.when((rem == MOE_ALIGN) | (rem == 3 * MOE_ALIGN))
    def _():
        fn(pl.multiple_of(base + rem - MOE_ALIGN, MOE_ALIGN), MOE_ALIGN)


def _expert_glu_kernel(ue_ref, uw_ref, ur_ref, x_ref, wg_ref, wl_ref, bg_ref, bl_ref, o_ref, wg_bf, wl_bf):
    rows = ur_ref[pl.program_id(0)]

    @pl.when(rows > 0)
    def _():
        wg_bf[...] = wg_ref[...].astype(BF16)
        wl_bf[...] = wl_ref[...].astype(BF16)

        def block(r0, size):
            x = x_ref[pl.ds(r0, size), :]
            gate = jnp.dot(x, wg_bf[...], preferred_element_type=F32) + bg_ref[...]
            lin = jnp.dot(x, wl_bf[...], preferred_element_type=F32) + bl_ref[...]
            gate = jnp.minimum(gate, SWIGLU_LIMIT)
            lin = jnp.clip(lin, -SWIGLU_LIMIT, SWIGLU_LIMIT)
            act = gate * jax.nn.sigmoid(SWIGLU_ALPHA * gate) * (lin + 1.0)
            o_ref[pl.ds(r0, size), :] = act.astype(o_ref.dtype)

        _for_row_blocks(rows, block)


def _expert_down_kernel(ue_ref, uw_ref, ur_ref, x_ref, w_ref, b_ref, o_ref, w_bf):
    rows = ur_ref[pl.program_id(0)]

    @pl.when(rows > 0)
    def _():
        w_bf[...] = w_ref[...].astype(BF16)

        def block(r0, size):
            x = x_ref[pl.ds(r0, size), :]
            o_ref[pl.ds(r0, size), :] = jnp.dot(x, w_bf[...], preferred_element_type=F32) + b_ref[...]

        _for_row_blocks(rows, block)


def _col(unit_rows, u, j, nt):
    return jnp.where(unit_rows[u] > 0, j, nt - 1)


def _out_block(unit_rows, u, j, n_units):
    used = unit_rows[u] > 0
    return jnp.where(used, u, n_units), jnp.where(used, j, 0)


def _expert_glu(x_sorted, w_gate_up, b_gate_up, unit_expert, unit_win, unit_rows):
    n_slots, d = x_sorted.shape
    n_e, _, two_h = w_gate_up.shape
    h = two_h // 2
    nt = h // MOE_TN
    n_units = unit_expert.shape[0]
    b3 = b_gate_up.reshape(n_e, 1, two_h)
    return pl.pallas_call(
        _expert_glu_kernel,
        grid_spec=pltpu.PrefetchScalarGridSpec(
            num_scalar_prefetch=3,
            grid=(n_units, nt),
            in_specs=[
                pl.BlockSpec((MOE_ROWS, d), lambda u, j, ue, uw, ur: (uw[u], 0)),
                pl.BlockSpec((None, d, MOE_TN), lambda u, j, ue, uw, ur: (ue[u], 0, _col(ur, u, j, nt))),
                pl.BlockSpec((None, d, MOE_TN), lambda u, j, ue, uw, ur: (ue[u], 0, nt + _col(ur, u, j, nt))),
                pl.BlockSpec((None, 1, MOE_TN), lambda u, j, ue, uw, ur: (ue[u], 0, _col(ur, u, j, nt))),
                pl.BlockSpec((None, 1, MOE_TN), lambda u, j, ue, uw, ur: (ue[u], 0, nt + _col(ur, u, j, nt))),
            ],
            out_specs=pl.BlockSpec((MOE_ROWS, MOE_TN), lambda u, j, ue, uw, ur: _out_block(ur, u, j, n_units)),
            scratch_shapes=[pltpu.VMEM((d, MOE_TN), BF16), pltpu.VMEM((d, MOE_TN), BF16)],
        ),
        out_shape=jax.ShapeDtypeStruct((n_slots + MOE_ROWS, h), BF16),
        compiler_params=_params("arbitrary", "arbitrary"),
        name="expert_glu",
    )(unit_expert, unit_win, unit_rows, x_sorted, w_gate_up, w_gate_up, b3, b3)


def _expert_down(act, w_down, b_down, unit_expert, unit_win, unit_rows):
    h = act.shape[1]
    n_e, _, d = w_down.shape
    n_units = unit_expert.shape[0]
    n_slots = (n_units + 1) * MOE_ROWS
    tn = MOE_TN_DOWN
    nt = d // tn
    b3 = b_down.reshape(n_e, 1, d)
    return pl.pallas_call(
        _expert_down_kernel,
        grid_spec=pltpu.PrefetchScalarGridSpec(
            num_scalar_prefetch=3,
            grid=(n_units, nt),
            in_specs=[
                pl.BlockSpec((MOE_ROWS, h), lambda u, j, ue, uw, ur: (uw[u], 0)),
                pl.BlockSpec((None, h, tn), lambda u, j, ue, uw, ur: (ue[u], 0, _col(ur, u, j, nt))),
                pl.BlockSpec((None, 1, tn), lambda u, j, ue, uw, ur: (ue[u], 0, _col(ur, u, j, nt))),
            ],
            out_specs=pl.BlockSpec((MOE_ROWS, tn), lambda u, j, ue, uw, ur: _out_block(ur, u, j, n_units)),
            scratch_shapes=[pltpu.VMEM((h, tn), BF16)],
        ),
        out_shape=jax.ShapeDtypeStruct((n_slots, d), F32),
        compiler_params=_params("arbitrary", "arbitrary"),
        name="expert_down",
    )(unit_expert, unit_win, unit_rows, act, w_down, b3)


def _combine_kernel(dest_ref, y_hbm, x_ref, r_ref, lg_ref, lb_ref, op_ref, os_ref, buf, sem,
                    *, tb, alpha, n_prompt_blocks):
    s = pl.program_id(0)
    n_steps = pl.num_programs(0)
    n_rows = tb * TOP_K

    def start(step, slot):
        base = step * n_rows

        def body(a, c):
            d = dest_ref[base + a]
            pltpu.make_async_copy(y_hbm.at[pl.ds(d, 1)], buf.at[slot, pl.ds(a, 1)], sem.at[slot]).start()
            return c
        lax.fori_loop(0, n_rows, body, 0, unroll=DMA_UNROLL)

    def wait(slot):
        pltpu.make_async_copy(y_hbm.at[pl.ds(0, n_rows)], buf.at[slot], sem.at[slot]).wait()

    @pl.when(s == 0)
    def _():
        start(0, 0)

    @pl.when(s + 1 < n_steps)
    def _():
        start(s + 1, (s + 1) % 2)

    slot = s % 2
    wait(slot)
    route = r_ref[...]
    f = jnp.zeros(x_ref.shape, F32)
    for k in range(TOP_K):
        f = f + buf[slot, k * tb:(k + 1) * tb] * route[:, ROUTE_GATE + k:ROUTE_GATE + k + 1]
    y = _ln(alpha * x_ref[...] + f, lg_ref[...], lb_ref[...])

    @pl.when(s < n_prompt_blocks)
    def _():
        op_ref[...] = y

    @pl.when(s >= n_prompt_blocks)
    def _():
        os_ref[...] = y


def _combine(dest, y_slots, x, route, ln_g, ln_b, alpha, tb, n_p):
    t, d = x.shape
    dest_flat = dest.reshape(t // tb, tb, TOP_K).transpose(0, 2, 1).reshape(-1)
    npb = n_p // tb
    prow, srow = _split_rows(npb)
    return pl.pallas_call(
        functools.partial(_combine_kernel, tb=tb, alpha=alpha, n_prompt_blocks=npb),
        grid_spec=pltpu.PrefetchScalarGridSpec(
            num_scalar_prefetch=1,
            grid=(t // tb,),
            in_specs=[
                pl.BlockSpec(memory_space=pl.ANY),
                pl.BlockSpec((tb, d), lambda s, dest: (s, 0)),
                pl.BlockSpec((tb, LANES), lambda s, dest: (s, 0)),
                pl.BlockSpec((1, d), lambda s, dest: (0, 0)),
                pl.BlockSpec((1, d), lambda s, dest: (0, 0)),
            ],
            out_specs=[
                pl.BlockSpec((tb, d), lambda s, dest: (prow(s), 0)),
                pl.BlockSpec((tb, d), lambda s, dest: (srow(s), 0)),
            ],
            scratch_shapes=[pltpu.VMEM((2, TOP_K * tb, d), F32), pltpu.SemaphoreType.DMA((2,))],
        ),
        out_shape=[
            jax.ShapeDtypeStruct((n_p, d), F32),
            jax.ShapeDtypeStruct((t - n_p, d), F32),
        ],
        compiler_params=_params("arbitrary"),
        name="combine_ln",
    )(dest_flat, y_slots, x, route, ln_g, ln_b)


def _rot_cols(w):
    half = ROPE_DIM // 2
    return jnp.concatenate([-w[..., half:], w[..., :half]], axis=-1)


def _rope_cs(pos):
    half = ROPE_DIM // 2
    inv_freq = ROPE_THETA ** (-jnp.arange(half, dtype=F32) / half)
    ang = pos.astype(F32)[:, None] * inv_freq[None, :]
    cos, sin = jnp.cos(ang), jnp.sin(ang)
    return jnp.concatenate([cos, cos, sin, sin], axis=-1)


def _kv_weight(w_uk, w_uv):
    wk = jnp.zeros((KV_IN, MLA_HEADS, QK_PAD), F32)
    wk = wk.at[:KV_LORA_RANK, :, :QK_NOPE_DIM].set(w_uk)
    eye = jnp.broadcast_to(jnp.eye(ROPE_DIM, dtype=F32)[:, None, :], (ROPE_DIM, MLA_HEADS, ROPE_DIM))
    wk = wk.at[KV_LORA_RANK:KV_LORA_RANK + ROPE_DIM, :, QK_NOPE_DIM:QK_NOPE_DIM + ROPE_DIM].set(eye)
    wv = jnp.zeros((KV_IN, MLA_HEADS, V_HEAD_DIM), F32).at[:KV_LORA_RANK].set(w_uv)
    return jnp.concatenate(
        [wk.reshape(KV_IN, MLA_HEADS * QK_PAD), wv.reshape(KV_IN, MLA_WIDTH)], axis=1).astype(BF16)


def _routing_tables(route, counts, n_tok):
    n_assign = n_tok * TOP_K
    n_units = n_assign // MOE_ROWS + N_EXPERTS
    n_slots = n_units * MOE_ROWS
    e = route[:, ROUTE_IDX:ROUTE_IDX + TOP_K].astype(jnp.int32)
    rank = route[:, ROUTE_RANK:ROUTE_RANK + TOP_K].astype(jnp.int32)
    units_per_e = (counts + MOE_ROWS - 1) // MOE_ROWS
    unit_end = jnp.cumsum(units_per_e)
    unit_start = unit_end - units_per_e
    total_units = unit_end[-1]
    dest = (unit_start[e] * MOE_ROWS + rank).reshape(-1)
    u = jnp.arange(n_units, dtype=jnp.int32)
    u_eff = jnp.minimum(u, total_units - 1)
    unit_expert = jnp.minimum(jnp.searchsorted(unit_end, u_eff, side="right"), N_EXPERTS - 1).astype(jnp.int32)
    rows = jnp.clip(counts[unit_expert] - (u_eff - unit_start[unit_expert]) * MOE_ROWS, 0, MOE_ROWS)
    unit_rows = jnp.where(u < total_units, rows, 0).astype(jnp.int32)
    unit_win = u_eff.astype(jnp.int32)
    slot_tok = jnp.zeros((n_slots,), jnp.int32).at[dest].set(jnp.arange(n_assign, dtype=jnp.int32) // TOP_K)
    sub_per_unit = MOE_ROWS // MOE_SUB
    n_steps = n_assign // MOE_SUB + N_EXPERTS
    blk_per_unit = (unit_rows + MOE_SUB - 1) // MOE_SUB
    blk_end = jnp.cumsum(blk_per_unit)
    n_active = blk_end[-1]
    st = jnp.minimum(jnp.arange(n_steps, dtype=jnp.int32), n_active - 1)
    st_unit = jnp.minimum(jnp.searchsorted(blk_end, st, side="right"), n_units - 1).astype(jnp.int32)
    act_blk = (st_unit * sub_per_unit + st - (blk_end - blk_per_unit)[st_unit]).astype(jnp.int32)
    return dest.astype(jnp.int32), slot_tok, act_blk, n_active.reshape(1).astype(jnp.int32), \
        unit_expert, unit_win, unit_rows, n_slots


def _pick_block(n, target, mult):
    best = None
    for b in range(mult, min(n, target) + 1, mult):
        if n % b == 0:
            best = b
    assert best is not None, (n, target, mult)
    return best


def kernel(x_prompt, x_sample, cache_kv_latent, cache_k_rope, w_in, q_a_norm_g, kv_a_norm_g, w_q_b, w_uk, w_uv, gmlp_norm_g, gmlp_norm_b, gmlp_w_s, gmlp_b_s, w_o, ln1_g, ln1_b, w_router, b_router, w_gate_up, b_gate_up, w_down, b_down, ln2_g, ln2_b):
    batch, seq, d_model = x_prompt.shape
    dbatch, dseq, _ = x_sample.shape
    depth, _, past, _ = cache_kv_latent.shape
    n_p, n_s = batch * seq, dbatch * dseq
    n_tok = n_p + n_s
    alpha = (2 * depth) ** 0.25
    assert n_p % GMLP_CHUNK == 0 and n_s % GMLP_CHUNK == 0 and GMLP_CHUNK % dseq == 0

    bm_big = _pick_block(n_tok, 1088, 16)
    bm_mid = _pick_block(n_tok, 512, 16)
    bm_split = _pick_block(math.gcd(n_p, n_s), 256, 16)
    tb = _pick_block(math.gcd(n_p, n_s), 128, 8)

    pos = jnp.concatenate([
        jnp.tile(jnp.arange(seq, dtype=jnp.int32), batch),
        jnp.tile(past + jnp.arange(dseq, dtype=jnp.int32), dbatch)])
    kcs = _rope_cs(pos)
    qtab = jnp.concatenate([jnp.ones((n_tok, QK_NOPE_DIM), F32), kcs], axis=-1) * ATTN_SCALE

    x_p = x_prompt.reshape(n_p, d_model)
    x_s = x_sample.reshape(n_s, d_model)

    lat_p, kr_p_all, lat_s, kr_s_all, gv_s_all = [], [], [], [], []
    for l in range(depth):
        w_kr = w_in[l][:, COL_KV:COL_KR]
        w_lat = jnp.concatenate([w_in[l][:, :COL_KR], _rot_cols(w_kr)], axis=1).astype(BF16)
        w_zu = w_in[l][:, COL_KR:COL_KR + GMLP_WIDTH].astype(BF16)
        w_zv = w_in[l][:, COL_KR + GMLP_WIDTH:].astype(BF16)
        wq = w_q_b[l]
        w_q = jnp.concatenate(
            [wq, _rot_cols(wq[..., QK_NOPE_DIM:])], axis=-1).reshape(Q_LORA_RANK, MLA_HEADS * QK_PAD).astype(BF16)
        w_kv = _kv_weight(w_uk[l], w_uv[l])
        cpos = jnp.arange(GMLP_CHUNK)
        cmask = (cpos[None, :] // CHUNK) <= (cpos[:, None] // CHUNK)
        w_sp_p = jnp.where(cmask[None], gmlp_w_s[l], 0.0)
        reps = GMLP_CHUNK // dseq
        dpos = jnp.arange(dseq)
        dmask = (dpos[None, :] // CHUNK) <= (dpos[:, None] // CHUNK)
        w_d = jnp.where(dmask[None], gmlp_w_s[l][:, :dseq, :dseq], 0.0)
        w_sp_s = jnp.einsum("ab,gij->gaibj", jnp.eye(reps, dtype=F32), w_d).reshape(
            GMLP_GROUPS, GMLP_CHUNK, GMLP_CHUNK)
        w_sets = jnp.stack([w_sp_p, w_sp_s]).astype(BF16)
        b_sets = jnp.stack([gmlp_b_s[l].T, jnp.tile(gmlp_b_s[l][:, :dseq], (1, reps)).T])
        w_o_bf = w_o[l].astype(BF16)
        w_r = jnp.zeros((d_model, LANES), F32).at[:, :N_EXPERTS].set(w_router[l])
        b_r = jnp.full((1, LANES), -jnp.inf, F32).at[0, :N_EXPERTS].set(b_router[l])

        cq, ckv, kpe, a_new, x_bf = _inproj_lat(
            x_p, x_s, w_lat, q_a_norm_g[l][None], kv_a_norm_g[l][None], kcs, bm_split)
        zu = _inproj_gelu(x_bf, w_zu, BF16, bm_big, 512)
        zv = _inproj_gelu(x_bf, w_zv, F32, bm_big, 512)
        q = _qproj(cq, w_q, qtab, bm_big)

        kv_p = _matmul(a_new, w_kv, BF16, bm_big, 1024, "kv_prompt")
        tq = min(512, seq)
        att_p = _attention(q, 0, kv_p, batch, seq, seq, seq, 0, tq, tq)

        sk_valid = past + dseq
        tk_s = 384
        sk = (sk_valid + tk_s - 1) // tk_s * tk_s
        cache_a = jnp.concatenate([
            cache_kv_latent[l], cache_k_rope[l],
            jnp.zeros((dbatch, past, KV_IN - KV_LORA_RANK - ROPE_DIM), F32)], axis=-1).astype(BF16)
        a_s = jnp.concatenate([
            cache_a, a_new[n_p:].reshape(dbatch, dseq, KV_IN),
            jnp.zeros((dbatch, sk - sk_valid, KV_IN), BF16)], axis=1).reshape(dbatch * sk, KV_IN)
        kv_s = _matmul(a_s, w_kv, BF16, sk, 1024, "kv_sample")
        att_s = _attention(q, n_p, kv_s, dbatch, dseq, sk, sk_valid, past, dseq, tk_s)
        att = jnp.concatenate([att_p, att_s], axis=0)

        g, vn = _gmlp(zu, zv, gmlp_norm_g[l][None], gmlp_norm_b[l][None], w_sets, b_sets, n_p // GMLP_CHUNK)
        x1 = _oproj(att, g, w_o_bf[:MLA_WIDTH], w_o_bf[MLA_WIDTH:], x_p, x_s, ln1_g[l][None], ln1_b[l][None],
                    alpha, bm_split, 1024)

        route, cnt = _router(x1, w_r, b_r, bm_mid)
        counts = cnt[0, :N_EXPERTS].astype(jnp.int32)
        dest, slot_tok, act_blk, n_active, unit_expert, unit_win, unit_rows, n_slots = _routing_tables(
            route, counts, n_tok)
        x_sorted = _dispatch(x1, slot_tok, act_blk, n_active, n_slots, MOE_SUB)
        act = _expert_glu(x_sorted, w_gate_up[l], b_gate_up[l], unit_expert, unit_win, unit_rows)
        y_slots = _expert_down(act, w_down[l], b_down[l], unit_expert, unit_win, unit_rows)
        x_p, x_s = _combine(dest, y_slots, x1, route, ln2_g[l][None], ln2_b[l][None], alpha, tb, n_p)

        lat_p.append(ckv[:n_p].reshape(batch, seq, KV_LORA_RANK))
        kr_p_all.append(kpe[:n_p].reshape(batch, seq, ROPE_DIM))
        lat_s.append(ckv[n_p:].reshape(dbatch, dseq, KV_LORA_RANK))
        kr_s_all.append(kpe[n_p:].reshape(dbatch, dseq, ROPE_DIM))
        gv_s_all.append(vn[n_p:].reshape(dbatch, dseq, GMLP_GROUPS, GMLP_GROUP_DIM))

    return (
        x_p.reshape(batch, seq, d_model),
        x_s.reshape(dbatch, dseq, d_model),
        jnp.stack(lat_p), jnp.stack(kr_p_all), jnp.stack(lat_s), jnp.stack(kr_s_all), jnp.stack(gv_s_all),
    )
```

```python
import functools
import math

import jax
import jax.numpy as jnp
from jax import lax
from jax.experimental import pallas as pl
from jax.experimental.pallas import tpu as pltpu

F32 = jnp.float32
BF16 = jnp.bfloat16

CHUNK = 64
MLA_HEADS = 16
QK_NOPE_DIM = 128
ROPE_DIM = 64
V_HEAD_DIM = 128
Q_LORA_RANK = 768
KV_LORA_RANK = 512
ROPE_THETA = 10000.0
ATTN_SCALE = (QK_NOPE_DIM + ROPE_DIM) ** -0.5
GMLP_GROUPS = 16
GMLP_GROUP_DIM = 128
GMLP_WIDTH = GMLP_GROUPS * GMLP_GROUP_DIM
GMLP_CHUNK = 128
MLA_WIDTH = MLA_HEADS * V_HEAD_DIM
COL_Q = Q_LORA_RANK
COL_KV = COL_Q + KV_LORA_RANK
COL_KR = COL_KV + ROPE_DIM
N_EXPERTS = 32
TOP_K = 4
SWIGLU_ALPHA = 1.702
SWIGLU_LIMIT = 7.0
LN_EPS = 1e-5
RMS_EPS = 1e-6

LANES = 128
QK_PAD = 256
KV_IN = KV_LORA_RANK + LANES
KV_OUT = MLA_HEADS * QK_PAD + MLA_HEADS * V_HEAD_DIM
MOE_ROWS = 1280
MOE_SUB = 256
MOE_BIG = 512
MOE_ALIGN = 128
MOE_KSPLIT = 4
MOE_TN = 256
MOE_TN_DOWN = 512
DMA_UNROLL = 8
VMEM_LIMIT = 56 * 1024 * 1024


def _params(*sem):
    return pltpu.CompilerParams(dimension_semantics=sem, vmem_limit_bytes=VMEM_LIMIT)


def _rms(x, g):
    return x * lax.rsqrt(jnp.mean(x * x, axis=-1, keepdims=True) + RMS_EPS) * g


def _ln(x, g, b):
    mu = jnp.mean(x, axis=-1, keepdims=True)
    d = x - mu
    var = jnp.mean(d * d, axis=-1, keepdims=True)
    return d * lax.rsqrt(var + LN_EPS) * g + b


def _inproj_lat_kernel(xp_ref, xs_ref, w_ref, gq_ref, gkv_ref, cs_ref, cq_ref, ckv_ref, kpe_ref, a_ref, xbf_ref,
                       *, n_prompt_blocks):
    i = pl.program_id(0)

    @pl.when(i < n_prompt_blocks)
    def _():
        xbf_ref[...] = xp_ref[...].astype(BF16)

    @pl.when(i >= n_prompt_blocks)
    def _():
        xbf_ref[...] = xs_ref[...].astype(BF16)

    p = jnp.dot(xbf_ref[...], w_ref[...], preferred_element_type=F32)
    cq_ref[...] = _rms(p[:, :COL_Q], gq_ref[...]).astype(BF16)
    ckv = _rms(p[:, COL_Q:COL_KV], gkv_ref[...])
    ckv_ref[...] = ckv
    t = p[:, COL_KV:] * cs_ref[...]
    r = t + pltpu.roll(t, ROPE_DIM, axis=1)
    kpe_ref[...] = r[:, :ROPE_DIM]
    a_ref[:, :KV_LORA_RANK] = ckv.astype(BF16)
    a_ref[:, KV_LORA_RANK:] = r.astype(BF16)


def _split_rows(n_prompt_blocks):
    prompt = lambda i: jnp.minimum(i, n_prompt_blocks - 1)
    sample = lambda i: jnp.maximum(i - n_prompt_blocks, 0)
    return prompt, sample


def _inproj_lat(x_p, x_s, w_lat, gq, gkv, kcs, bm):
    (n_p, d), n_s = x_p.shape, x_s.shape[0]
    t = n_p + n_s
    n = w_lat.shape[1]
    npb = n_p // bm
    prow, srow = _split_rows(npb)
    return pl.pallas_call(
        functools.partial(_inproj_lat_kernel, n_prompt_blocks=npb),
        grid=(t // bm,),
        in_specs=[
            pl.BlockSpec((bm, d), lambda i: (prow(i), 0)),
            pl.BlockSpec((bm, d), lambda i: (srow(i), 0)),
            pl.BlockSpec((d, n), lambda i: (0, 0)),
            pl.BlockSpec((1, COL_Q), lambda i: (0, 0)),
            pl.BlockSpec((1, KV_LORA_RANK), lambda i: (0, 0)),
            pl.BlockSpec((bm, LANES), lambda i: (i, 0)),
        ],
        out_specs=[
            pl.BlockSpec((bm, COL_Q), lambda i: (i, 0)),
            pl.BlockSpec((bm, KV_LORA_RANK), lambda i: (i, 0)),
            pl.BlockSpec((bm, ROPE_DIM), lambda i: (i, 0)),
            pl.BlockSpec((bm, KV_IN), lambda i: (i, 0)),
            pl.BlockSpec((bm, d), lambda i: (i, 0)),
        ],
        out_shape=[
            jax.ShapeDtypeStruct((t, COL_Q), BF16),
            jax.ShapeDtypeStruct((t, KV_LORA_RANK), F32),
            jax.ShapeDtypeStruct((t, ROPE_DIM), F32),
            jax.ShapeDtypeStruct((t, KV_IN), BF16),
            jax.ShapeDtypeStruct((t, d), BF16),
        ],
        compiler_params=_params("parallel"),
        name="inproj_lat",
    )(x_p, x_s, w_lat, gq, gkv, kcs)


def _inproj_gelu_kernel(x_ref, w_ref, o_ref):
    p = jnp.dot(x_ref[...], w_ref[...], preferred_element_type=F32)
    o_ref[...] = (0.5 * p * (1.0 + lax.erf(p * (2.0 ** -0.5)))).astype(o_ref.dtype)


def _inproj_gelu(x_bf, w, out_dtype, bm, bn):
    t, d = x_bf.shape
    n = w.shape[1]
    return pl.pallas_call(
        _inproj_gelu_kernel,
        grid=(t // bm, n // bn),
        in_specs=[
            pl.BlockSpec((bm, d), lambda i, j: (i, 0)),
            pl.BlockSpec((d, bn), lambda i, j: (0, j)),
        ],
        out_specs=pl.BlockSpec((bm, bn), lambda i, j: (i, j)),
        out_shape=jax.ShapeDtypeStruct((t, n), out_dtype),
        compiler_params=_params("parallel", "parallel"),
        name="inproj_gelu",
    )(x_bf, w)


def _qproj_kernel(c_ref, w_ref, tab_ref, o_ref, *, heads):
    c = c_ref[...]
    tab = tab_ref[...]
    for h in range(heads):
        c0 = h * QK_PAD
        p = jnp.dot(c, w_ref[:, c0:c0 + QK_PAD], preferred_element_type=F32) * tab
        o_ref[:, c0:c0 + QK_NOPE_DIM] = p[:, :QK_NOPE_DIM].astype(BF16)
        t = p[:, QK_NOPE_DIM:]
        o_ref[:, c0 + QK_NOPE_DIM:c0 + QK_PAD] = (t + pltpu.roll(t, ROPE_DIM, axis=1)).astype(BF16)


def _qproj(cq, w_q, qtab, bm, heads=4):
    t, r = cq.shape
    return pl.pallas_call(
        functools.partial(_qproj_kernel, heads=heads),
        grid=(t // bm, MLA_HEADS // heads),
        in_specs=[
            pl.BlockSpec((bm, r), lambda i, h: (i, 0)),
            pl.BlockSpec((r, heads * QK_PAD), lambda i, h: (0, h)),
            pl.BlockSpec((bm, QK_PAD), lambda i, h: (i, 0)),
        ],
        out_specs=pl.BlockSpec((bm, heads * QK_PAD), lambda i, h: (i, h)),
        out_shape=jax.ShapeDtypeStruct((t, MLA_HEADS * QK_PAD), BF16),
        compiler_params=_params("parallel", "parallel"),
        name="qproj",
    )(cq, w_q, qtab)


def _matmul_kernel(x_ref, w_ref, o_ref):
    o_ref[...] = jnp.dot(x_ref[...], w_ref[...], preferred_element_type=F32).astype(o_ref.dtype)


def _matmul(x, w, out_dtype, bm, bn, name):
    m, k = x.shape
    n = w.shape[1]
    return pl.pallas_call(
        _matmul_kernel,
        grid=(m // bm, n // bn),
        in_specs=[
            pl.BlockSpec((bm, k), lambda i, j: (i, 0)),
            pl.BlockSpec((k, bn), lambda i, j: (0, j)),
        ],
        out_specs=pl.BlockSpec((bm, bn), lambda i, j: (i, j)),
        out_shape=jax.ShapeDtypeStruct((m, n), out_dtype),
        compiler_params=_params("parallel", "parallel"),
        name=name,
    )(x, w)


def _attn_kernel(q_ref, k_ref, v_ref, o_ref, *, tq, tk, sk_valid, q_pos0, heads):
    i = pl.program_id(2)
    q_lo = q_pos0 + i * tq
    seen_by_all = jnp.minimum((q_lo // CHUNK + 1) * CHUNK, sk_valid)
    seen_by_any = jnp.minimum(((q_lo + tq - 1) // CHUNK + 1) * CHUNK, sk_valid)
    n_full = seen_by_all // tk
    n_all = (seen_by_any + tk - 1) // tk

    def step(masked):
        def body(j, carry):
            k0 = pl.multiple_of(j * tk, tk)
            if masked:
                qpos = q_lo + lax.broadcasted_iota(jnp.int32, (tq, tk), 0)
                kpos = k0 + lax.broadcasted_iota(jnp.int32, (tq, tk), 1)
                ok = (kpos // CHUNK <= qpos // CHUNK) & (kpos < sk_valid)
            out = []
            for h, (m, l, acc) in enumerate(carry):
                q = q_ref[:, h * QK_PAD:(h + 1) * QK_PAD]
                k = k_ref[pl.ds(k0, tk), h * QK_PAD:(h + 1) * QK_PAD]
                s = lax.dot_general(q, k, (((1,), (1,)), ((), ())), preferred_element_type=F32)
                if masked:
                    s = jnp.where(ok, s, -jnp.inf)
                m_new = jnp.maximum(m, jnp.max(s, axis=-1, keepdims=True))
                a = jnp.exp(m - m_new)
                p = jnp.exp(s - m_new)
                l = a * l + jnp.sum(p, axis=-1, keepdims=True)
                v = v_ref[pl.ds(k0, tk), h * V_HEAD_DIM:(h + 1) * V_HEAD_DIM]
                acc = a * acc + jnp.dot(p.astype(BF16), v, preferred_element_type=F32)
                out.append((m_new, l, acc))
            return tuple(out)

        return body

    init = tuple(
        (jnp.full((tq, 1), -1e30, F32), jnp.zeros((tq, 1), F32), jnp.zeros((tq, V_HEAD_DIM), F32))
        for _ in range(heads))
    carry = lax.fori_loop(0, n_full, step(False), init)
    carry = lax.fori_loop(n_full, n_all, step(True), carry)
    for h, (_, l, acc) in enumerate(carry):
        o_ref[:, h * V_HEAD_DIM:(h + 1) * V_HEAD_DIM] = (acc / l).astype(BF16)


def _attention(q, q_row0, kv, batch, sq, sk, sk_valid, q_pos0, tq, tk, heads):
    nq = sq // tq
    qb0 = q_row0 // tq
    v_col0 = MLA_HEADS * QK_PAD // (heads * V_HEAD_DIM)
    return pl.pallas_call(
        functools.partial(_attn_kernel, tq=tq, tk=tk, sk_valid=sk_valid, q_pos0=q_pos0, heads=heads),
        grid=(batch, MLA_HEADS // heads, nq),
        in_specs=[
            pl.BlockSpec((tq, heads * QK_PAD), lambda b, h, i: (qb0 + b * nq + i, h)),
            pl.BlockSpec((sk, heads * QK_PAD), lambda b, h, i: (b, h)),
            pl.BlockSpec((sk, heads * V_HEAD_DIM), lambda b, h, i: (b, v_col0 + h)),
        ],
        out_specs=pl.BlockSpec((tq, heads * V_HEAD_DIM), lambda b, h, i: (b * nq + i, h)),
        out_shape=jax.ShapeDtypeStruct((batch * sq, MLA_WIDTH), BF16),
        compiler_params=_params("parallel", "parallel", "parallel"),
        name="attention",
    )(q, kv, kv)


def _gmlp_kernel(u_ref, v_ref, g_ref, b_ref, w_ref, bs_ref, o_ref, vn_ref):
    vn = _ln(v_ref[...], g_ref[...], b_ref[...])
    vn_ref[...] = vn
    bs = bs_ref[...]
    for g in range(GMLP_GROUPS):
        cols = slice(g * GMLP_GROUP_DIM, (g + 1) * GMLP_GROUP_DIM)
        mix = jnp.dot(w_ref[g], vn[:, cols].astype(BF16), preferred_element_type=F32)
        mix = mix + bs[:, g:g + 1]
        o_ref[:, cols] = (u_ref[:, cols].astype(F32) * mix).astype(BF16)


def _gmlp(zu, zv, g, b, w_sets, b_sets, n_prompt_chunks):
    t, n = zu.shape
    c = GMLP_CHUNK
    pick = lambda i: jnp.where(i >= n_prompt_chunks, 1, 0)
    return pl.pallas_call(
        _gmlp_kernel,
        grid=(t // c,),
        in_specs=[
            pl.BlockSpec((c, n), lambda i: (i, 0)),
            pl.BlockSpec((c, n), lambda i: (i, 0)),
            pl.BlockSpec((1, n), lambda i: (0, 0)),
            pl.BlockSpec((1, n), lambda i: (0, 0)),
            pl.BlockSpec((None, GMLP_GROUPS, c, c), lambda i: (pick(i), 0, 0, 0)),
            pl.BlockSpec((None, c, GMLP_GROUPS), lambda i: (pick(i), 0, 0)),
        ],
        out_specs=[
            pl.BlockSpec((c, n), lambda i: (i, 0)),
            pl.BlockSpec((c, n), lambda i: (i, 0)),
        ],
        out_shape=[
            jax.ShapeDtypeStruct((t, n), BF16),
            jax.ShapeDtypeStruct((t, n), F32),
        ],
        compiler_params=_params("parallel"),
        name="gmlp",
    )(zu, zv, g, b, w_sets, b_sets)


def _oproj_kernel(att_ref, g_ref, wa_ref, wg_ref, xp_ref, xs_ref, lg_ref, lb_ref, o_ref, acc_ref,
                  *, bn, alpha, n_prompt_blocks):
    i = pl.program_id(0)
    j = pl.program_id(1)
    nj, bm, _ = acc_ref.shape
    acc_ref[j] = (
        jnp.dot(att_ref[...], wa_ref[...], preferred_element_type=F32)
        + jnp.dot(g_ref[...], wg_ref[...], preferred_element_type=F32)
    )

    def finish(x_ref):
        d = nj * bn
        s1 = jnp.zeros((bm, 1), F32)
        for c in range(nj):
            y = alpha * x_ref[:, c * bn:(c + 1) * bn] + acc_ref[c]
            acc_ref[c] = y
            s1 = s1 + jnp.sum(y, axis=-1, keepdims=True)
        mu = s1 / d
        s2 = jnp.zeros((bm, 1), F32)
        for c in range(nj):
            dev = acc_ref[c] - mu
            s2 = s2 + jnp.sum(dev * dev, axis=-1, keepdims=True)
        inv = lax.rsqrt(s2 / d + LN_EPS)
        for c in range(nj):
            cols = slice(c * bn, (c + 1) * bn)
            o_ref[:, cols] = (acc_ref[c] - mu) * inv * lg_ref[:, cols] + lb_ref[:, cols]

    @pl.when((j == nj - 1) & (i < n_prompt_blocks))
    def _():
        finish(xp_ref)

    @pl.when((j == nj - 1) & (i >= n_prompt_blocks))
    def _():
        finish(xs_ref)


def _oproj(att, g, w_att, w_g, x_p, x_s, ln_g, ln_b, alpha, bm, bn):
    t, ka = att.shape
    kg = g.shape[1]
    n_p, d = x_p.shape
    npb = n_p // bm
    prow, srow = _split_rows(npb)
    return pl.pallas_call(
        functools.partial(_oproj_kernel, bn=bn, alpha=alpha, n_prompt_blocks=npb),
        grid=(t // bm, d // bn),
        in_specs=[
            pl.BlockSpec((bm, ka), lambda i, j: (i, 0)),
            pl.BlockSpec((bm, kg), lambda i, j: (i, 0)),
            pl.BlockSpec((ka, bn), lambda i, j: (0, j)),
            pl.BlockSpec((kg, bn), lambda i, j: (0, j)),
            pl.BlockSpec((bm, d), lambda i, j: (prow(i), 0)),
            pl.BlockSpec((bm, d), lambda i, j: (srow(i), 0)),
            pl.BlockSpec((1, d), lambda i, j: (0, 0)),
            pl.BlockSpec((1, d), lambda i, j: (0, 0)),
        ],
        out_specs=pl.BlockSpec((bm, d), lambda i, j: (i, 0)),
        out_shape=jax.ShapeDtypeStruct((t, d), F32),
        scratch_shapes=[pltpu.VMEM((d // bn, bm, bn), F32)],
        compiler_params=_params("parallel", "arbitrary"),
        name="oproj_ln",
    )(att, g, w_att, w_g, x_p, x_s, ln_g, ln_b)


ROUTE_IDX, ROUTE_GATE, ROUTE_RANK = 0, TOP_K, 2 * TOP_K


def _router_kernel(x_ref, w_ref, b_ref, o_ref, cnt_out_ref, cnt_ref, *, bm):
    @pl.when(pl.program_id(0) == 0)
    def _():
        cnt_ref[...] = jnp.zeros_like(cnt_ref)

    logits = jnp.dot(x_ref[...], w_ref[...], preferred_element_type=F32,
                     precision=lax.Precision.HIGHEST) + b_ref[...]
    lane = lax.broadcasted_iota(jnp.int32, (bm, LANES), 1)
    work = logits
    vals, hits = [], []
    out = jnp.zeros((bm, LANES), F32)
    for k in range(TOP_K):
        m = jnp.max(work, axis=-1, keepdims=True)
        idx = jnp.min(jnp.where(work == m, lane, LANES), axis=-1, keepdims=True)
        hit = lane == idx
        work = jnp.where(hit, -jnp.inf, work)
        vals.append(m)
        hits.append(hit)
        out = jnp.where(lane == ROUTE_IDX + k, idx.astype(F32), out)
    exps = [jnp.exp(v - vals[0]) for v in vals]
    den = exps[0] + exps[1] + exps[2] + exps[3]
    member = jnp.zeros((bm, LANES), F32)
    for hit in hits:
        member = member + hit.astype(F32)
    row = lax.broadcasted_iota(jnp.int32, (bm, bm), 0)
    col = lax.broadcasted_iota(jnp.int32, (bm, bm), 1)
    before = (col < row).astype(BF16)
    prefix = jnp.dot(before, member.astype(BF16), preferred_element_type=F32) + cnt_ref[0:1, :]
    for k in range(TOP_K):
        out = jnp.where(lane == ROUTE_GATE + k, exps[k] / den, out)
        rank = jnp.sum(jnp.where(hits[k], prefix, 0.0), axis=-1, keepdims=True)
        out = jnp.where(lane == ROUTE_RANK + k, rank, out)
    o_ref[...] = out
    cnt_ref[0:1, :] = cnt_ref[0:1, :] + jnp.sum(member, axis=0, keepdims=True)
    cnt_out_ref[...] = cnt_ref[...]


def _router(x, w_pad, b_pad, bm):
    t, d = x.shape
    return pl.pallas_call(
        functools.partial(_router_kernel, bm=bm),
        grid=(t // bm,),
        in_specs=[
            pl.BlockSpec((bm, d), lambda i: (i, 0)),
            pl.BlockSpec((d, LANES), lambda i: (0, 0)),
            pl.BlockSpec((1, LANES), lambda i: (0, 0)),
        ],
        out_specs=[
            pl.BlockSpec((bm, LANES), lambda i: (i, 0)),
            pl.BlockSpec((8, LANES), lambda i: (0, 0)),
        ],
        out_shape=[
            jax.ShapeDtypeStruct((t, LANES), F32),
            jax.ShapeDtypeStruct((8, LANES), F32),
        ],
        scratch_shapes=[pltpu.VMEM((8, LANES), F32)],
        compiler_params=_params("arbitrary"),
        name="router",
    )(x, w_pad, b_pad)


def _dispatch_kernel(blk_ref, nact_ref, tok_ref, x_hbm, o_ref, buf, sem, *, rows):
    s = pl.program_id(0)
    n_active = nact_ref[0]

    def start(step, slot):
        base = blk_ref[step] * rows

        def body(r, c):
            tok = tok_ref[base + r]
            pltpu.make_async_copy(x_hbm.at[pl.ds(tok, 1)], buf.at[slot, pl.ds(r, 1)], sem.at[slot]).start()
            return c
        lax.fori_loop(0, rows, body, 0, unroll=DMA_UNROLL)

    def wait(step, slot):
        pltpu.make_async_copy(x_hbm.at[pl.ds(0, rows)], buf.at[slot], sem.at[slot]).wait()

    @pl.when((s == 0) & (n_active > 0))
    def _():
        start(0, 0)

    @pl.when(s + 1 < n_active)
    def _():
        start(s + 1, (s + 1) % 2)

    @pl.when(s < n_active)
    def _():
        wait(s, s % 2)
        o_ref[...] = buf[s % 2].astype(BF16)


def _dispatch(x, slot_tok, act_blk, n_active, n_slots, rows):
    t, d = x.shape
    n_steps = act_blk.shape[0]
    return pl.pallas_call(
        functools.partial(_dispatch_kernel, rows=rows),
        grid_spec=pltpu.PrefetchScalarGridSpec(
            num_scalar_prefetch=3,
            grid=(n_steps,),
            in_specs=[pl.BlockSpec(memory_space=pl.ANY)],
            out_specs=pl.BlockSpec((rows, d), lambda s, blk, nact, tok: (blk[s], 0)),
            scratch_shapes=[pltpu.VMEM((2, rows, d), F32), pltpu.SemaphoreType.DMA((2,))],
        ),
        out_shape=jax.ShapeDtypeStruct((n_slots, d), BF16),
        compiler_params=_params("arbitrary"),
        name="dispatch",
    )(act_blk, n_active, slot_tok, x)


def _for_row_blocks(rows, fn, cast, fused_first):
    padded = (rows + MOE_ALIGN - 1) // MOE_ALIGN * MOE_ALIGN
    n_big = padded // MOE_BIG

    @pl.when(n_big > 0)
    def _():
        fused_first()

    @pl.when(n_big == 0)
    def _():
        cast()

    def body(s, c):
        fn(pl.multiple_of(s * MOE_BIG, MOE_BIG), MOE_BIG)
        return c

    lax.fori_loop(1, n_big, body, 0)
    base = n_big * MOE_BIG
    rem = padded - base

    @pl.when(rem >= 2 * MOE_ALIGN)
    def _():
        fn(pl.multiple_of(base, 2 * MOE_ALIGN), 2 * MOE_ALIGN)

    @pl.when((rem == MOE_ALIGN) | (rem == 3 * MOE_ALIGN))
    def _():
        fn(pl.multiple_of(base + rem - MOE_ALIGN, MOE_ALIGN), MOE_ALIGN)


def _k_chunks(k):
    step = k // MOE_KSPLIT
    return [slice(c * step, (c + 1) * step) for c in range(MOE_KSPLIT)]


def _expert_glu_kernel(ue_ref, uw_ref, ur_ref, x_ref, wg_ref, wl_ref, bg_ref, bl_ref, o_ref, wg_bf, wl_bf):
    rows = ur_ref[pl.program_id(0)]

    def cast():
        wg_bf[...] = wg_ref[...].astype(BF16)
        wl_bf[...] = wl_ref[...].astype(BF16)

    def finish(r0, size, gate, lin):
        gate = jnp.minimum(gate + bg_ref[...], SWIGLU_LIMIT)
        lin = jnp.clip(lin + bl_ref[...], -SWIGLU_LIMIT, SWIGLU_LIMIT)
        act = gate * jax.nn.sigmoid(SWIGLU_ALPHA * gate) * (lin + 1.0)
        o_ref[pl.ds(r0, size), :] = act.astype(o_ref.dtype)

    def block(r0, size):
        x = x_ref[pl.ds(r0, size), :]
        finish(r0, size, jnp.dot(x, wg_bf[...], preferred_element_type=F32),
               jnp.dot(x, wl_bf[...], preferred_element_type=F32))

    def fused_first():
        gate = lin = None
        for ks in _k_chunks(x_ref.shape[1]):
            wg_c = wg_ref[ks, :].astype(BF16)
            wl_c = wl_ref[ks, :].astype(BF16)
            wg_bf[ks, :] = wg_c
            wl_bf[ks, :] = wl_c
            x = x_ref[0:MOE_BIG, ks]
            pg = jnp.dot(x, wg_c, preferred_element_type=F32)
            pn = jnp.dot(x, wl_c, preferred_element_type=F32)
            gate = pg if gate is None else gate + pg
            lin = pn if lin is None else lin + pn
        finish(0, MOE_BIG, gate, lin)

    @pl.when(rows > 0)
    def _():
        _for_row_blocks(rows, block, cast, fused_first)


def _expert_down_kernel(ue_ref, uw_ref, ur_ref, x_ref, w_ref, b_ref, o_ref, w_bf):
    rows = ur_ref[pl.program_id(0)]

    def cast():
        w_bf[...] = w_ref[...].astype(BF16)

    def block(r0, size):
        x = x_ref[pl.ds(r0, size), :]
        o_ref[pl.ds(r0, size), :] = jnp.dot(x, w_bf[...], preferred_element_type=F32) + b_ref[...]

    def fused_first():
        acc = None
        for ks in _k_chunks(x_ref.shape[1]):
            w_c = w_ref[ks, :].astype(BF16)
            w_bf[ks, :] = w_c
            p = jnp.dot(x_ref[0:MOE_BIG, ks], w_c, preferred_element_type=F32)
            acc = p if acc is None else acc + p
        o_ref[0:MOE_BIG, :] = acc + b_ref[...]

    @pl.when(rows > 0)
    def _():
        _for_row_blocks(rows, block, cast, fused_first)


def _col(unit_rows, u, j, nt):
    return jnp.where(unit_rows[u] > 0, j, nt - 1)


def _out_block(unit_rows, u, j, n_units):
    used = unit_rows[u] > 0
    return jnp.where(used, u, n_units), jnp.where(used, j, 0)


def _expert_glu(x_sorted, w_gate_up, b_gate_up, unit_expert, unit_win, unit_rows):
    n_slots, d = x_sorted.shape
    n_e, _, two_h = w_gate_up.shape
    h = two_h // 2
    nt = h // MOE_TN
    n_units = unit_expert.shape[0]
    b3 = b_gate_up.reshape(n_e, 1, two_h)
    return pl.pallas_call(
        _expert_glu_kernel,
        grid_spec=pltpu.PrefetchScalarGridSpec(
            num_scalar_prefetch=3,
            grid=(n_units, nt),
            in_specs=[
                pl.BlockSpec((MOE_ROWS, d), lambda u, j, ue, uw, ur: (uw[u], 0)),
                pl.BlockSpec((None, d, MOE_TN), lambda u, j, ue, uw, ur: (ue[u], 0, _col(ur, u, j, nt))),
                pl.BlockSpec((None, d, MOE_TN), lambda u, j, ue, uw, ur: (ue[u], 0, nt + _col(ur, u, j, nt))),
                pl.BlockSpec((None, 1, MOE_TN), lambda u, j, ue, uw, ur: (ue[u], 0, _col(ur, u, j, nt))),
                pl.BlockSpec((None, 1, MOE_TN), lambda u, j, ue, uw, ur: (ue[u], 0, nt + _col(ur, u, j, nt))),
            ],
            out_specs=pl.BlockSpec((MOE_ROWS, MOE_TN), lambda u, j, ue, uw, ur: _out_block(ur, u, j, n_units)),
            scratch_shapes=[pltpu.VMEM((d, MOE_TN), BF16), pltpu.VMEM((d, MOE_TN), BF16)],
        ),
        out_shape=jax.ShapeDtypeStruct((n_slots + MOE_ROWS, h), BF16),
        compiler_params=_params("arbitrary", "arbitrary"),
        name="expert_glu",
    )(unit_expert, unit_win, unit_rows, x_sorted, w_gate_up, w_gate_up, b3, b3)


def _expert_down(act, w_down, b_down, unit_expert, unit_win, unit_rows):
    h = act.shape[1]
    n_e, _, d = w_down.shape
    n_units = unit_expert.shape[0]
    n_slots = (n_units + 1) * MOE_ROWS
    tn = MOE_TN_DOWN
    nt = d // tn
    b3 = b_down.reshape(n_e, 1, d)
    return pl.pallas_call(
        _expert_down_kernel,
        grid_spec=pltpu.PrefetchScalarGridSpec(
            num_scalar_prefetch=3,
            grid=(n_units, nt),
            in_specs=[
                pl.BlockSpec((MOE_ROWS, h), lambda u, j, ue, uw, ur: (uw[u], 0)),
                pl.BlockSpec((None, h, tn), lambda u, j, ue, uw, ur: (ue[u], 0, _col(ur, u, j, nt))),
                pl.BlockSpec((None, 1, tn), lambda u, j, ue, uw, ur: (ue[u], 0, _col(ur, u, j, nt))),
            ],
            out_specs=pl.BlockSpec((MOE_ROWS, tn), lambda u, j, ue, uw, ur: _out_block(ur, u, j, n_units)),
            scratch_shapes=[pltpu.VMEM((h, tn), BF16)],
        ),
        out_shape=jax.ShapeDtypeStruct((n_slots, d), F32),
        compiler_params=_params("arbitrary", "arbitrary"),
        name="expert_down",
    )(unit_expert, unit_win, unit_rows, act, w_down, b3)


def _combine_kernel(dest_ref, y_hbm, x_ref, r_ref, lg_ref, lb_ref, op_ref, os_ref, buf, sem,
                    *, tb, alpha, n_prompt_blocks):
    s = pl.program_id(0)
    n_steps = pl.num_programs(0)
    n_rows = tb * TOP_K

    def start(step, slot):
        base = step * n_rows

        def body(a, c):
            d = dest_ref[base + a]
            pltpu.make_async_copy(y_hbm.at[pl.ds(d, 1)], buf.at[slot, pl.ds(a, 1)], sem.at[slot]).start()
            return c
        lax.fori_loop(0, n_rows, body, 0, unroll=DMA_UNROLL)

    def wait(slot):
        pltpu.make_async_copy(y_hbm.at[pl.ds(0, n_rows)], buf.at[slot], sem.at[slot]).wait()

    @pl.when(s == 0)
    def _():
        start(0, 0)

    @pl.when(s + 1 < n_steps)
    def _():
        start(s + 1, (s + 1) % 2)

    slot = s % 2
    wait(slot)
    route = r_ref[...]
    f = jnp.zeros(x_ref.shape, F32)
    for k in range(TOP_K):
        f = f + buf[slot, k * tb:(k + 1) * tb] * route[:, ROUTE_GATE + k:ROUTE_GATE + k + 1]
    y = _ln(alpha * x_ref[...] + f, lg_ref[...], lb_ref[...])

    @pl.when(s < n_prompt_blocks)
    def _():
        op_ref[...] = y

    @pl.when(s >= n_prompt_blocks)
    def _():
        os_ref[...] = y


def _combine(dest, y_slots, x, route, ln_g, ln_b, alpha, tb, n_p):
    t, d = x.shape
    dest_flat = dest.reshape(t // tb, tb, TOP_K).transpose(0, 2, 1).reshape(-1)
    npb = n_p // tb
    prow, srow = _split_rows(npb)
    return pl.pallas_call(
        functools.partial(_combine_kernel, tb=tb, alpha=alpha, n_prompt_blocks=npb),
        grid_spec=pltpu.PrefetchScalarGridSpec(
            num_scalar_prefetch=1,
            grid=(t // tb,),
            in_specs=[
                pl.BlockSpec(memory_space=pl.ANY),
                pl.BlockSpec((tb, d), lambda s, dest: (s, 0)),
                pl.BlockSpec((tb, LANES), lambda s, dest: (s, 0)),
                pl.BlockSpec((1, d), lambda s, dest: (0, 0)),
                pl.BlockSpec((1, d), lambda s, dest: (0, 0)),
            ],
            out_specs=[
                pl.BlockSpec((tb, d), lambda s, dest: (prow(s), 0)),
                pl.BlockSpec((tb, d), lambda s, dest: (srow(s), 0)),
            ],
            scratch_shapes=[pltpu.VMEM((2, TOP_K * tb, d), F32), pltpu.SemaphoreType.DMA((2,))],
        ),
        out_shape=[
            jax.ShapeDtypeStruct((n_p, d), F32),
            jax.ShapeDtypeStruct((t - n_p, d), F32),
        ],
        compiler_params=_params("arbitrary"),
        name="combine_ln",
    )(dest_flat, y_slots, x, route, ln_g, ln_b)


def _rot_cols(w):
    half = ROPE_DIM // 2
    return jnp.concatenate([-w[..., half:], w[..., :half]], axis=-1)


def _rope_cs(pos):
    half = ROPE_DIM // 2
    inv_freq = ROPE_THETA ** (-jnp.arange(half, dtype=F32) / half)
    ang = pos.astype(F32)[:, None] * inv_freq[None, :]
    cos, sin = jnp.cos(ang), jnp.sin(ang)
    return jnp.concatenate([cos, cos, sin, sin], axis=-1)


def _kv_weight(w_uk, w_uv):
    wk = jnp.zeros((KV_IN, MLA_HEADS, QK_PAD), F32)
    wk = wk.at[:KV_LORA_RANK, :, :QK_NOPE_DIM].set(w_uk)
    eye = jnp.broadcast_to(jnp.eye(ROPE_DIM, dtype=F32)[:, None, :], (ROPE_DIM, MLA_HEADS, ROPE_DIM))
    wk = wk.at[KV_LORA_RANK:KV_LORA_RANK + ROPE_DIM, :, QK_NOPE_DIM:QK_NOPE_DIM + ROPE_DIM].set(eye)
    wv = jnp.zeros((KV_IN, MLA_HEADS, V_HEAD_DIM), F32).at[:KV_LORA_RANK].set(w_uv)
    return jnp.concatenate(
        [wk.reshape(KV_IN, MLA_HEADS * QK_PAD), wv.reshape(KV_IN, MLA_WIDTH)], axis=1).astype(BF16)


def _routing_tables(route, counts, n_tok):
    n_assign = n_tok * TOP_K
    n_units = n_assign // MOE_ROWS + N_EXPERTS
    n_slots = n_units * MOE_ROWS
    e = route[:, ROUTE_IDX:ROUTE_IDX + TOP_K].astype(jnp.int32)
    rank = route[:, ROUTE_RANK:ROUTE_RANK + TOP_K].astype(jnp.int32)
    units_per_e = (counts + MOE_ROWS - 1) // MOE_ROWS
    unit_end = jnp.cumsum(units_per_e)
    unit_start = unit_end - units_per_e
    total_units = unit_end[-1]
    dest = (unit_start[e] * MOE_ROWS + rank).reshape(-1)
    u = jnp.arange(n_units, dtype=jnp.int32)
    u_eff = jnp.minimum(u, jnp.maximum(total_units - 1, 0))
    unit_expert = jnp.minimum(jnp.searchsorted(unit_end, u_eff, side="right"), N_EXPERTS - 1).astype(jnp.int32)
    rows = jnp.clip(counts[unit_expert] - (u_eff - unit_start[unit_expert]) * MOE_ROWS, 0, MOE_ROWS)
    unit_rows = jnp.where(u < total_units, rows, 0).astype(jnp.int32)
    unit_win = u_eff.astype(jnp.int32)
    slot_tok = jnp.zeros((n_slots,), jnp.int32).at[dest].set(jnp.arange(n_assign, dtype=jnp.int32) // TOP_K)
    sub_per_unit = MOE_ROWS // MOE_SUB
    n_steps = n_assign // MOE_SUB + N_EXPERTS
    blk_per_unit = (unit_rows + MOE_SUB - 1) // MOE_SUB
    blk_end = jnp.cumsum(blk_per_unit)
    n_active = blk_end[-1]
    st = jnp.minimum(jnp.arange(n_steps, dtype=jnp.int32), jnp.maximum(n_active - 1, 0))
    st_unit = jnp.minimum(jnp.searchsorted(blk_end, st, side="right"), n_units - 1).astype(jnp.int32)
    act_blk = (st_unit * sub_per_unit + st - (blk_end - blk_per_unit)[st_unit]).astype(jnp.int32)
    return dest.astype(jnp.int32), slot_tok, act_blk, n_active.reshape(1).astype(jnp.int32), \
        unit_expert, unit_win, unit_rows, n_slots


def _pick_block(n, target, mult):
    best = None
    for b in range(mult, min(n, target) + 1, mult):
        if n % b == 0:
            best = b
    assert best is not None, (n, target, mult)
    return best


def kernel(x_prompt, x_sample, cache_kv_latent, cache_k_rope, w_in, q_a_norm_g, kv_a_norm_g, w_q_b, w_uk, w_uv, gmlp_norm_g, gmlp_norm_b, gmlp_w_s, gmlp_b_s, w_o, ln1_g, ln1_b, w_router, b_router, w_gate_up, b_gate_up, w_down, b_down, ln2_g, ln2_b):
    batch, seq, d_model = x_prompt.shape
    dbatch, dseq, _ = x_sample.shape
    depth, _, past, _ = cache_kv_latent.shape
    n_p, n_s = batch * seq, dbatch * dseq
    n_tok = n_p + n_s
    alpha = (2 * depth) ** 0.25
    assert n_p % GMLP_CHUNK == 0 and n_s % GMLP_CHUNK == 0 and GMLP_CHUNK % dseq == 0

    bm_big = _pick_block(n_tok, 1088, 16)
    bm_mid = _pick_block(n_tok, 512, 16)
    bm_split = _pick_block(math.gcd(n_p, n_s), 256, 16)
    tb = _pick_block(math.gcd(n_p, n_s), 128, 8)

    pos = jnp.concatenate([
        jnp.tile(jnp.arange(seq, dtype=jnp.int32), batch),
        jnp.tile(past + jnp.arange(dseq, dtype=jnp.int32), dbatch)])
    kcs = _rope_cs(pos)
    qtab = jnp.concatenate([jnp.ones((n_tok, QK_NOPE_DIM), F32), kcs], axis=-1) * ATTN_SCALE

    x_p = x_prompt.reshape(n_p, d_model)
    x_s = x_sample.reshape(n_s, d_model)

    lat_p, kr_p_all, lat_s, kr_s_all, gv_s_all = [], [], [], [], []
    for l in range(depth):
        w_kr = w_in[l][:, COL_KV:COL_KR]
        w_lat = jnp.concatenate([w_in[l][:, :COL_KR], _rot_cols(w_kr)], axis=1).astype(BF16)
        w_zu = w_in[l][:, COL_KR:COL_KR + GMLP_WIDTH].astype(BF16)
        w_zv = w_in[l][:, COL_KR + GMLP_WIDTH:].astype(BF16)
        wq = w_q_b[l]
        w_q = jnp.concatenate(
            [wq, _rot_cols(wq[..., QK_NOPE_DIM:])], axis=-1).reshape(Q_LORA_RANK, MLA_HEADS * QK_PAD).astype(BF16)
        w_kv = _kv_weight(w_uk[l], w_uv[l])
        cpos = jnp.arange(GMLP_CHUNK)
        cmask = (cpos[None, :] // CHUNK) <= (cpos[:, None] // CHUNK)
        w_sp_p = jnp.where(cmask[None], gmlp_w_s[l], 0.0)
        reps = GMLP_CHUNK // dseq
        dpos = jnp.arange(dseq)
        dmask = (dpos[None, :] // CHUNK) <= (dpos[:, None] // CHUNK)
        w_d = jnp.where(dmask[None], gmlp_w_s[l][:, :dseq, :dseq], 0.0)
        w_sp_s = jnp.einsum("ab,gij->gaibj", jnp.eye(reps, dtype=F32), w_d).reshape(
            GMLP_GROUPS, GMLP_CHUNK, GMLP_CHUNK)
        w_sets = jnp.stack([w_sp_p, w_sp_s]).astype(BF16)
        b_sets = jnp.stack([gmlp_b_s[l].T, jnp.tile(gmlp_b_s[l][:, :dseq], (1, reps)).T])
        w_o_bf = w_o[l].astype(BF16)
        w_r = jnp.zeros((d_model, LANES), F32).at[:, :N_EXPERTS].set(w_router[l])
        b_r = jnp.full((1, LANES), -jnp.inf, F32).at[0, :N_EXPERTS].set(b_router[l])

        cq, ckv, kpe, a_new, x_bf = _inproj_lat(
            x_p, x_s, w_lat, q_a_norm_g[l][None], kv_a_norm_g[l][None], kcs, bm_split)
        zu = _inproj_gelu(x_bf, w_zu, BF16, bm_big, 512)
        zv = _inproj_gelu(x_bf, w_zv, F32, bm_big, 512)
        q = _qproj(cq, w_q, qtab, bm_big)

        kv_p = _matmul(a_new, w_kv, BF16, bm_big, 1024, "kv_prompt")
        tq = min(512, seq)
        att_p = _attention(q, 0, kv_p, batch, seq, seq, seq, 0, tq, tq, 2)

        sk_valid = past + dseq
        tk_s = 384
        sk = (sk_valid + tk_s - 1) // tk_s * tk_s
        cache_a = jnp.concatenate([
            cache_kv_latent[l], cache_k_rope[l],
            jnp.zeros((dbatch, past, KV_IN - KV_LORA_RANK - ROPE_DIM), F32)], axis=-1).astype(BF16)
        a_s = jnp.concatenate([
            cache_a, a_new[n_p:].reshape(dbatch, dseq, KV_IN),
            jnp.zeros((dbatch, sk - sk_valid, KV_IN), BF16)], axis=1).reshape(dbatch * sk, KV_IN)
        kv_s = _matmul(a_s, w_kv, BF16, sk, 1024, "kv_sample")
        att_s = _attention(q, n_p, kv_s, dbatch, dseq, sk, sk_valid, past, dseq, tk_s, 4)
        att = jnp.concatenate([att_p, att_s], axis=0)

        g, vn = _gmlp(zu, zv, gmlp_norm_g[l][None], gmlp_norm_b[l][None], w_sets, b_sets, n_p // GMLP_CHUNK)
        x1 = _oproj(att, g, w_o_bf[:MLA_WIDTH], w_o_bf[MLA_WIDTH:], x_p, x_s, ln1_g[l][None], ln1_b[l][None],
                    alpha, bm_split, 1024)

        route, cnt = _router(x1, w_r, b_r, bm_mid)
        counts = cnt[0, :N_EXPERTS].astype(jnp.int32)
        dest, slot_tok, act_blk, n_active, unit_expert, unit_win, unit_rows, n_slots = _routing_tables(
            route, counts, n_tok)
        x_sorted = _dispatch(x1, slot_tok, act_blk, n_active, n_slots, MOE_SUB)
        act = _expert_glu(x_sorted, w_gate_up[l], b_gate_up[l], unit_expert, unit_win, unit_rows)
        y_slots = _expert_down(act, w_down[l], b_down[l], unit_expert, unit_win, unit_rows)
        x_p, x_s = _combine(dest, y_slots, x1, route, ln2_g[l][None], ln2_b[l][None], alpha, tb, n_p)

        lat_p.append(ckv[:n_p].reshape(batch, seq, KV_LORA_RANK))
        kr_p_all.append(kpe[:n_p].reshape(batch, seq, ROPE_DIM))
        lat_s.append(ckv[n_p:].reshape(dbatch, dseq, KV_LORA_RANK))
        kr_s_all.append(kpe[n_p:].reshape(dbatch, dseq, ROPE_DIM))
        gv_s_all.append(vn[n_p:].reshape(dbatch, dseq, GMLP_GROUPS, GMLP_GROUP_DIM))

    return (
        x_p.reshape(batch, seq, d_model),
        x_s.reshape(dbatch, dseq, d_model),
        jnp.stack(lat_p), jnp.stack(kr_p_all), jnp.stack(lat_s), jnp.stack(kr_s_all), jnp.stack(gv_s_all),
    )
```

```python
import functools
import math

import jax
import jax.numpy as jnp
from jax import lax
from jax.experimental import pallas as pl
from jax.experimental.pallas import tpu as pltpu

F32 = jnp.float32
BF16 = jnp.bfloat16

CHUNK = 64
MLA_HEADS = 16
QK_NOPE_DIM = 128
ROPE_DIM = 64
V_HEAD_DIM = 128
Q_LORA_RANK = 768
KV_LORA_RANK = 512
ROPE_THETA = 10000.0
ATTN_SCALE = (QK_NOPE_DIM + ROPE_DIM) ** -0.5
GMLP_GROUPS = 16
GMLP_GROUP_DIM = 128
GMLP_WIDTH = GMLP_GROUPS * GMLP_GROUP_DIM
GMLP_CHUNK = 128
MLA_WIDTH = MLA_HEADS * V_HEAD_DIM
COL_Q = Q_LORA_RANK
COL_KV = COL_Q + KV_LORA_RANK
COL_KR = COL_KV + ROPE_DIM
N_EXPERTS = 32
TOP_K = 4
SWIGLU_ALPHA = 1.702
SWIGLU_LIMIT = 7.0
LN_EPS = 1e-5
RMS_EPS = 1e-6

LANES = 128
QK_PAD = 256
KV_IN = KV_LORA_RANK + LANES
KV_OUT = MLA_HEADS * QK_PAD + MLA_HEADS * V_HEAD_DIM
MOE_ROWS = 1280
MOE_SUB = 256
MOE_BIG = 512
MOE_ALIGN = 128
MOE_KSPLIT = 4
MOE_TN = 256
MOE_TN_DOWN = 512
DMA_UNROLL = 8
VMEM_LIMIT = 56 * 1024 * 1024


def _params(*sem):
    return pltpu.CompilerParams(dimension_semantics=sem, vmem_limit_bytes=VMEM_LIMIT)


def _rms(x, g):
    return x * lax.rsqrt(jnp.mean(x * x, axis=-1, keepdims=True) + RMS_EPS) * g


def _ln(x, g, b):
    mu = jnp.mean(x, axis=-1, keepdims=True)
    d = x - mu
    var = jnp.mean(d * d, axis=-1, keepdims=True)
    return d * lax.rsqrt(var + LN_EPS) * g + b


def _inproj_lat_kernel(xp_ref, xs_ref, w_ref, gq_ref, gkv_ref, cs_ref, cq_ref, ckv_ref, kpe_ref, a_ref, xbf_ref,
                       *, n_prompt_blocks):
    i = pl.program_id(0)

    @pl.when(i < n_prompt_blocks)
    def _():
        xbf_ref[...] = xp_ref[...].astype(BF16)

    @pl.when(i >= n_prompt_blocks)
    def _():
        xbf_ref[...] = xs_ref[...].astype(BF16)

    p = jnp.dot(xbf_ref[...], w_ref[...], preferred_element_type=F32)
    cq_ref[...] = _rms(p[:, :COL_Q], gq_ref[...]).astype(BF16)
    ckv = _rms(p[:, COL_Q:COL_KV], gkv_ref[...])
    ckv_ref[...] = ckv
    t = p[:, COL_KV:] * cs_ref[...]
    r = t + pltpu.roll(t, ROPE_DIM, axis=1)
    kpe_ref[...] = r[:, :ROPE_DIM]
    a_ref[:, :KV_LORA_RANK] = ckv.astype(BF16)
    lane = lax.broadcasted_iota(jnp.int32, r.shape, 1)
    a_ref[:, KV_LORA_RANK:] = jnp.where(lane < ROPE_DIM, r, 0.0).astype(BF16)


def _split_rows(n_prompt_blocks):
    prompt = lambda i: jnp.minimum(i, n_prompt_blocks - 1)
    sample = lambda i: jnp.maximum(i - n_prompt_blocks, 0)
    return prompt, sample


def _inproj_lat(x_p, x_s, w_lat, gq, gkv, kcs, bm):
    (n_p, d), n_s = x_p.shape, x_s.shape[0]
    t = n_p + n_s
    n = w_lat.shape[1]
    npb = n_p // bm
    prow, srow = _split_rows(npb)
    return pl.pallas_call(
        functools.partial(_inproj_lat_kernel, n_prompt_blocks=npb),
        grid=(t // bm,),
        in_specs=[
            pl.BlockSpec((bm, d), lambda i: (prow(i), 0)),
            pl.BlockSpec((bm, d), lambda i: (srow(i), 0)),
            pl.BlockSpec((d, n), lambda i: (0, 0)),
            pl.BlockSpec((1, COL_Q), lambda i: (0, 0)),
            pl.BlockSpec((1, KV_LORA_RANK), lambda i: (0, 0)),
            pl.BlockSpec((bm, LANES), lambda i: (i, 0)),
        ],
        out_specs=[
            pl.BlockSpec((bm, COL_Q), lambda i: (i, 0)),
            pl.BlockSpec((bm, KV_LORA_RANK), lambda i: (i, 0)),
            pl.BlockSpec((bm, ROPE_DIM), lambda i: (i, 0)),
            pl.BlockSpec((bm, KV_IN), lambda i: (i, 0)),
            pl.BlockSpec((bm, d), lambda i: (i, 0)),
        ],
        out_shape=[
            jax.ShapeDtypeStruct((t, COL_Q), BF16),
            jax.ShapeDtypeStruct((t, KV_LORA_RANK), F32),
            jax.ShapeDtypeStruct((t, ROPE_DIM), F32),
            jax.ShapeDtypeStruct((t, KV_IN), BF16),
            jax.ShapeDtypeStruct((t, d), BF16),
        ],
        compiler_params=_params("parallel"),
        name="inproj_lat",
    )(x_p, x_s, w_lat, gq, gkv, kcs)


def _inproj_gelu_kernel(x_ref, w_ref, o_ref):
    p = jnp.dot(x_ref[...], w_ref[...], preferred_element_type=F32)
    o_ref[...] = (0.5 * p * (1.0 + lax.erf(p * (2.0 ** -0.5)))).astype(o_ref.dtype)


def _inproj_gelu(x_bf, w, out_dtype, bm, bn):
    t, d = x_bf.shape
    n = w.shape[1]
    return pl.pallas_call(
        _inproj_gelu_kernel,
        grid=(t // bm, n // bn),
        in_specs=[
            pl.BlockSpec((bm, d), lambda i, j: (i, 0)),
            pl.BlockSpec((d, bn), lambda i, j: (0, j)),
        ],
        out_specs=pl.BlockSpec((bm, bn), lambda i, j: (i, j)),
        out_shape=jax.ShapeDtypeStruct((t, n), out_dtype),
        compiler_params=_params("parallel", "parallel"),
        name="inproj_gelu",
    )(x_bf, w)


def _qproj_kernel(c_ref, w_ref, tab_ref, o_ref, *, heads):
    c = c_ref[...]
    tab = tab_ref[...]
    for h in range(heads):
        c0 = h * QK_PAD
        p = jnp.dot(c, w_ref[:, c0:c0 + QK_PAD], preferred_element_type=F32) * tab
        o_ref[:, c0:c0 + QK_NOPE_DIM] = p[:, :QK_NOPE_DIM].astype(BF16)
        t = p[:, QK_NOPE_DIM:]
        o_ref[:, c0 + QK_NOPE_DIM:c0 + QK_PAD] = (t + pltpu.roll(t, ROPE_DIM, axis=1)).astype(BF16)


def _qproj(cq, w_q, qtab, bm, heads=4):
    t, r = cq.shape
    return pl.pallas_call(
        functools.partial(_qproj_kernel, heads=heads),
        grid=(t // bm, MLA_HEADS // heads),
        in_specs=[
            pl.BlockSpec((bm, r), lambda i, h: (i, 0)),
            pl.BlockSpec((r, heads * QK_PAD), lambda i, h: (0, h)),
            pl.BlockSpec((bm, QK_PAD), lambda i, h: (i, 0)),
        ],
        out_specs=pl.BlockSpec((bm, heads * QK_PAD), lambda i, h: (i, h)),
        out_shape=jax.ShapeDtypeStruct((t, MLA_HEADS * QK_PAD), BF16),
        compiler_params=_params("parallel", "parallel"),
        name="qproj",
    )(cq, w_q, qtab)


def _matmul_kernel(x_ref, w_ref, o_ref):
    o_ref[...] = jnp.dot(x_ref[...], w_ref[...], preferred_element_type=F32).astype(o_ref.dtype)


def _matmul(x, w, out_dtype, bm, bn, name):
    m, k = x.shape
    n = w.shape[1]
    return pl.pallas_call(
        _matmul_kernel,
        grid=(m // bm, n // bn),
        in_specs=[
            pl.BlockSpec((bm, k), lambda i, j: (i, 0)),
            pl.BlockSpec((k, bn), lambda i, j: (0, j)),
        ],
        out_specs=pl.BlockSpec((bm, bn), lambda i, j: (i, j)),
        out_shape=jax.ShapeDtypeStruct((m, n), out_dtype),
        compiler_params=_params("parallel", "parallel"),
        name=name,
    )(x, w)


def _attn_kernel(q_ref, k_ref, v_ref, o_ref, *, tq, tk, sk_valid, q_pos0, heads):
    i = pl.program_id(2)
    q_lo = q_pos0 + i * tq
    seen_by_all = jnp.minimum((q_lo // CHUNK + 1) * CHUNK, sk_valid)
    seen_by_any = jnp.minimum(((q_lo + tq - 1) // CHUNK + 1) * CHUNK, sk_valid)
    n_full = seen_by_all // tk
    n_all = (seen_by_any + tk - 1) // tk

    def step(masked):
        def body(j, carry):
            k0 = pl.multiple_of(j * tk, tk)
            if masked:
                qpos = q_lo + lax.broadcasted_iota(jnp.int32, (tq, tk), 0)
                kpos = k0 + lax.broadcasted_iota(jnp.int32, (tq, tk), 1)
                ok = (kpos // CHUNK <= qpos // CHUNK) & (kpos < sk_valid)
            out = []
            for h, (m, l, acc) in enumerate(carry):
                q = q_ref[:, h * QK_PAD:(h + 1) * QK_PAD]
                k = k_ref[pl.ds(k0, tk), h * QK_PAD:(h + 1) * QK_PAD]
                s = lax.dot_general(q, k, (((1,), (1,)), ((), ())), preferred_element_type=F32)
                if masked:
                    s = jnp.where(ok, s, -jnp.inf)
                m_new = jnp.maximum(m, jnp.max(s, axis=-1, keepdims=True))
                a = jnp.exp(m - m_new)
                p = jnp.exp(s - m_new)
                l = a * l + jnp.sum(p, axis=-1, keepdims=True)
                v = v_ref[pl.ds(k0, tk), h * V_HEAD_DIM:(h + 1) * V_HEAD_DIM]
                acc = a * acc + jnp.dot(p.astype(BF16), v, preferred_element_type=F32)
                out.append((m_new, l, acc))
            return tuple(out)

        return body

    init = tuple(
        (jnp.full((tq, 1), -1e30, F32), jnp.zeros((tq, 1), F32), jnp.zeros((tq, V_HEAD_DIM), F32))
        for _ in range(heads))
    carry = lax.fori_loop(0, n_full, step(False), init)
    carry = lax.fori_loop(n_full, n_all, step(True), carry)
    for h, (_, l, acc) in enumerate(carry):
        o_ref[:, h * V_HEAD_DIM:(h + 1) * V_HEAD_DIM] = (acc / l).astype(BF16)


def _attention(q, q_row0, kv, batch, sq, sk, sk_valid, q_pos0, tq, tk, heads):
    nq = sq // tq
    qb0 = q_row0 // tq
    v_col0 = MLA_HEADS * QK_PAD // (heads * V_HEAD_DIM)
    return pl.pallas_call(
        functools.partial(_attn_kernel, tq=tq, tk=tk, sk_valid=sk_valid, q_pos0=q_pos0, heads=heads),
        grid=(batch, MLA_HEADS // heads, nq),
        in_specs=[
            pl.BlockSpec((tq, heads * QK_PAD), lambda b, h, i: (qb0 + b * nq + i, h)),
            pl.BlockSpec((sk, heads * QK_PAD), lambda b, h, i: (b, h)),
            pl.BlockSpec((sk, heads * V_HEAD_DIM), lambda b, h, i: (b, v_col0 + h)),
        ],
        out_specs=pl.BlockSpec((tq, heads * V_HEAD_DIM), lambda b, h, i: (b * nq + i, h)),
        out_shape=jax.ShapeDtypeStruct((batch * sq, MLA_WIDTH), BF16),
        compiler_params=_params("parallel", "parallel", "parallel"),
        name="attention",
    )(q, kv, kv)


def _attn_latent_kernel(q_ref, a_ref, wuk_ref, wuv_ref, o_ref, qc_ref, *, sq, sk_valid, q_pos0):
    for h in range(MLA_HEADS):
        rows = slice(h * sq, (h + 1) * sq)
        c0 = h * QK_PAD
        q_lat = jnp.dot(q_ref[:, c0:c0 + QK_NOPE_DIM], wuk_ref[h], preferred_element_type=F32)
        qc_ref[rows, :KV_LORA_RANK] = q_lat.astype(BF16)
        qc_ref[rows, KV_LORA_RANK:] = q_ref[:, c0 + QK_NOPE_DIM:c0 + QK_PAD]
    a = a_ref[...]
    s = lax.dot_general(qc_ref[...], a, (((1,), (1,)), ((), ())), preferred_element_type=F32)
    n_rows, sk = s.shape
    qpos = q_pos0 + lax.broadcasted_iota(jnp.int32, (n_rows, sk), 0) % sq
    kpos = lax.broadcasted_iota(jnp.int32, (n_rows, sk), 1)
    ok = (kpos // CHUNK <= qpos // CHUNK) & (kpos < sk_valid)
    s = jnp.where(ok, s, -jnp.inf)
    p = jnp.exp(s - jnp.max(s, axis=-1, keepdims=True))
    l = jnp.sum(p, axis=-1, keepdims=True)
    o_lat = jnp.dot(p.astype(BF16), a[:, :KV_LORA_RANK], preferred_element_type=F32) / l
    for h in range(MLA_HEADS):
        o = jnp.dot(o_lat[h * sq:(h + 1) * sq].astype(BF16), wuv_ref[h], preferred_element_type=F32)
        o_ref[:, h * V_HEAD_DIM:(h + 1) * V_HEAD_DIM] = o.astype(BF16)


def _attention_latent(q, q_row0, a, wuk_t, wuv, batch, sq, sk, sk_valid, q_pos0):
    qb0 = q_row0 // sq
    return pl.pallas_call(
        functools.partial(_attn_latent_kernel, sq=sq, sk_valid=sk_valid, q_pos0=q_pos0),
        grid=(batch,),
        in_specs=[
            pl.BlockSpec((sq, MLA_HEADS * QK_PAD), lambda b: (qb0 + b, 0)),
            pl.BlockSpec((sk, KV_IN), lambda b: (b, 0)),
            pl.BlockSpec((MLA_HEADS, QK_NOPE_DIM, KV_LORA_RANK), lambda b: (0, 0, 0)),
            pl.BlockSpec((MLA_HEADS, KV_LORA_RANK, V_HEAD_DIM), lambda b: (0, 0, 0)),
        ],
        out_specs=pl.BlockSpec((sq, MLA_WIDTH), lambda b: (b, 0)),
        out_shape=jax.ShapeDtypeStruct((batch * sq, MLA_WIDTH), BF16),
        scratch_shapes=[pltpu.VMEM((MLA_HEADS * sq, KV_IN), BF16)],
        compiler_params=_params("parallel"),
        name="attention_latent",
    )(q, a, wuk_t, wuv)


def _gmlp_kernel(u_ref, v_ref, g_ref, b_ref, w_ref, bs_ref, o_ref, vn_ref):
    vn = _ln(v_ref[...], g_ref[...], b_ref[...])
    vn_ref[...] = vn
    bs = bs_ref[...]
    for g in range(GMLP_GROUPS):
        cols = slice(g * GMLP_GROUP_DIM, (g + 1) * GMLP_GROUP_DIM)
        mix = jnp.dot(w_ref[g], vn[:, cols].astype(BF16), preferred_element_type=F32)
        mix = mix + bs[:, g:g + 1]
        o_ref[:, cols] = (u_ref[:, cols].astype(F32) * mix).astype(BF16)


def _gmlp(zu, zv, g, b, w_sets, b_sets, n_prompt_chunks):
    t, n = zu.shape
    c = GMLP_CHUNK
    pick = lambda i: jnp.where(i >= n_prompt_chunks, 1, 0)
    return pl.pallas_call(
        _gmlp_kernel,
        grid=(t // c,),
        in_specs=[
            pl.BlockSpec((c, n), lambda i: (i, 0)),
            pl.BlockSpec((c, n), lambda i: (i, 0)),
            pl.BlockSpec((1, n), lambda i: (0, 0)),
            pl.BlockSpec((1, n), lambda i: (0, 0)),
            pl.BlockSpec((None, GMLP_GROUPS, c, c), lambda i: (pick(i), 0, 0, 0)),
            pl.BlockSpec((None, c, GMLP_GROUPS), lambda i: (pick(i), 0, 0)),
        ],
        out_specs=[
            pl.BlockSpec((c, n), lambda i: (i, 0)),
            pl.BlockSpec((c, n), lambda i: (i, 0)),
        ],
        out_shape=[
            jax.ShapeDtypeStruct((t, n), BF16),
            jax.ShapeDtypeStruct((t, n), F32),
        ],
        compiler_params=_params("parallel"),
        name="gmlp",
    )(zu, zv, g, b, w_sets, b_sets)


def _oproj_kernel(att_ref, g_ref, wa_ref, wg_ref, xp_ref, xs_ref, lg_ref, lb_ref, o_ref, acc_ref,
                  *, bn, alpha, n_prompt_blocks):
    i = pl.program_id(0)
    j = pl.program_id(1)
    nj, bm, _ = acc_ref.shape
    acc_ref[j] = (
        jnp.dot(att_ref[...], wa_ref[...], preferred_element_type=F32)
        + jnp.dot(g_ref[...], wg_ref[...], preferred_element_type=F32)
    )

    def finish(x_ref):
        d = nj * bn
        s1 = jnp.zeros((bm, 1), F32)
        for c in range(nj):
            y = alpha * x_ref[:, c * bn:(c + 1) * bn] + acc_ref[c]
            acc_ref[c] = y
            s1 = s1 + jnp.sum(y, axis=-1, keepdims=True)
        mu = s1 / d
        s2 = jnp.zeros((bm, 1), F32)
        for c in range(nj):
            dev = acc_ref[c] - mu
            s2 = s2 + jnp.sum(dev * dev, axis=-1, keepdims=True)
        inv = lax.rsqrt(s2 / d + LN_EPS)
        for c in range(nj):
            cols = slice(c * bn, (c + 1) * bn)
            o_ref[:, cols] = (acc_ref[c] - mu) * inv * lg_ref[:, cols] + lb_ref[:, cols]

    @pl.when((j == nj - 1) & (i < n_prompt_blocks))
    def _():
        finish(xp_ref)

    @pl.when((j == nj - 1) & (i >= n_prompt_blocks))
    def _():
        finish(xs_ref)


def _oproj(att, g, w_att, w_g, x_p, x_s, ln_g, ln_b, alpha, bm, bn):
    t, ka = att.shape
    kg = g.shape[1]
    n_p, d = x_p.shape
    npb = n_p // bm
    prow, srow = _split_rows(npb)
    return pl.pallas_call(
        functools.partial(_oproj_kernel, bn=bn, alpha=alpha, n_prompt_blocks=npb),
        grid=(t // bm, d // bn),
        in_specs=[
            pl.BlockSpec((bm, ka), lambda i, j: (i, 0)),
            pl.BlockSpec((bm, kg), lambda i, j: (i, 0)),
            pl.BlockSpec((ka, bn), lambda i, j: (0, j)),
            pl.BlockSpec((kg, bn), lambda i, j: (0, j)),
            pl.BlockSpec((bm, d), lambda i, j: (prow(i), 0)),
            pl.BlockSpec((bm, d), lambda i, j: (srow(i), 0)),
            pl.BlockSpec((1, d), lambda i, j: (0, 0)),
            pl.BlockSpec((1, d), lambda i, j: (0, 0)),
        ],
        out_specs=pl.BlockSpec((bm, d), lambda i, j: (i, 0)),
        out_shape=jax.ShapeDtypeStruct((t, d), F32),
        scratch_shapes=[pltpu.VMEM((d // bn, bm, bn), F32)],
        compiler_params=_params("parallel", "arbitrary"),
        name="oproj_ln",
    )(att, g, w_att, w_g, x_p, x_s, ln_g, ln_b)


ROUTE_IDX, ROUTE_GATE, ROUTE_RANK = 0, TOP_K, 2 * TOP_K


def _router_kernel(x_ref, w_ref, b_ref, o_ref, cnt_out_ref, cnt_ref, *, bm):
    @pl.when(pl.program_id(0) == 0)
    def _():
        cnt_ref[...] = jnp.zeros_like(cnt_ref)

    logits = jnp.dot(x_ref[...], w_ref[...], preferred_element_type=F32,
                     precision=lax.Precision.HIGHEST) + b_ref[...]
    lane = lax.broadcasted_iota(jnp.int32, (bm, LANES), 1)
    work = logits
    vals, hits = [], []
    out = jnp.zeros((bm, LANES), F32)
    for k in range(TOP_K):
        m = jnp.max(work, axis=-1, keepdims=True)
        idx = jnp.min(jnp.where(work == m, lane, LANES), axis=-1, keepdims=True)
        hit = lane == idx
        work = jnp.where(hit, -jnp.inf, work)
        vals.append(m)
        hits.append(hit)
        out = jnp.where(lane == ROUTE_IDX + k, idx.astype(F32), out)
    exps = [jnp.exp(v - vals[0]) for v in vals]
    den = exps[0] + exps[1] + exps[2] + exps[3]
    member = jnp.zeros((bm, LANES), F32)
    for hit in hits:
        member = member + hit.astype(F32)
    row = lax.broadcasted_iota(jnp.int32, (bm, bm), 0)
    col = lax.broadcasted_iota(jnp.int32, (bm, bm), 1)
    before = (col < row).astype(BF16)
    prefix = jnp.dot(before, member.astype(BF16), preferred_element_type=F32) + cnt_ref[0:1, :]
    for k in range(TOP_K):
        out = jnp.where(lane == ROUTE_GATE + k, exps[k] / den, out)
        rank = jnp.sum(jnp.where(hits[k], prefix, 0.0), axis=-1, keepdims=True)
        out = jnp.where(lane == ROUTE_RANK + k, rank, out)
    o_ref[...] = out
    cnt_ref[0:1, :] = cnt_ref[0:1, :] + jnp.sum(member, axis=0, keepdims=True)
    cnt_out_ref[...] = cnt_ref[...]


def _router(x, w_pad, b_pad, bm):
    t, d = x.shape
    return pl.pallas_call(
        functools.partial(_router_kernel, bm=bm),
        grid=(t // bm,),
        in_specs=[
            pl.BlockSpec((bm, d), lambda i: (i, 0)),
            pl.BlockSpec((d, LANES), lambda i: (0, 0)),
            pl.BlockSpec((1, LANES), lambda i: (0, 0)),
        ],
        out_specs=[
            pl.BlockSpec((bm, LANES), lambda i: (i, 0)),
            pl.BlockSpec((8, LANES), lambda i: (0, 0)),
        ],
        out_shape=[
            jax.ShapeDtypeStruct((t, LANES), F32),
            jax.ShapeDtypeStruct((8, LANES), F32),
        ],
        scratch_shapes=[pltpu.VMEM((8, LANES), F32)],
        compiler_params=_params("arbitrary"),
        name="router",
    )(x, w_pad, b_pad)


def _dispatch_kernel(blk_ref, nact_ref, tok_ref, x_hbm, o_ref, buf, sem, *, rows):
    s = pl.program_id(0)
    n_active = nact_ref[0]

    def start(step, slot):
        base = blk_ref[step] * rows

        def body(r, c):
            tok = tok_ref[base + r]
            pltpu.make_async_copy(x_hbm.at[pl.ds(tok, 1)], buf.at[slot, pl.ds(r, 1)], sem.at[slot]).start()
            return c
        lax.fori_loop(0, rows, body, 0, unroll=DMA_UNROLL)

    def wait(step, slot):
        pltpu.make_async_copy(x_hbm.at[pl.ds(0, rows)], buf.at[slot], sem.at[slot]).wait()

    @pl.when((s == 0) & (n_active > 0))
    def _():
        start(0, 0)

    @pl.when(s + 1 < n_active)
    def _():
        start(s + 1, (s + 1) % 2)

    @pl.when(s < n_active)
    def _():
        wait(s, s % 2)
        o_ref[...] = buf[s % 2].astype(BF16)


def _dispatch(x, slot_tok, act_blk, n_active, n_slots, rows):
    t, d = x.shape
    n_steps = act_blk.shape[0]
    return pl.pallas_call(
        functools.partial(_dispatch_kernel, rows=rows),
        grid_spec=pltpu.PrefetchScalarGridSpec(
            num_scalar_prefetch=3,
            grid=(n_steps,),
            in_specs=[pl.BlockSpec(memory_space=pl.ANY)],
            out_specs=pl.BlockSpec((rows, d), lambda s, blk, nact, tok: (blk[s], 0)),
            scratch_shapes=[pltpu.VMEM((2, rows, d), F32), pltpu.SemaphoreType.DMA((2,))],
        ),
        out_shape=jax.ShapeDtypeStruct((n_slots, d), BF16),
        compiler_params=_params("arbitrary"),
        name="dispatch",
    )(act_blk, n_active, slot_tok, x)


def _for_row_blocks(rows, fn, cast, fused_first):
    padded = (rows + MOE_ALIGN - 1) // MOE_ALIGN * MOE_ALIGN
    n_big = padded // MOE_BIG

    @pl.when(n_big > 0)
    def _():
        fused_first()

    @pl.when(n_big == 0)
    def _():
        cast()

    def body(s, c):
        fn(pl.multiple_of(s * MOE_BIG, MOE_BIG), MOE_BIG)
        return c

    lax.fori_loop(1, n_big, body, 0)
    base = n_big * MOE_BIG
    rem = padded - base

    @pl.when(rem >= 2 * MOE_ALIGN)
    def _():
        fn(pl.multiple_of(base, 2 * MOE_ALIGN), 2 * MOE_ALIGN)

    @pl.when((rem == MOE_ALIGN) | (rem == 3 * MOE_ALIGN))
    def _():
        fn(pl.multiple_of(base + rem - MOE_ALIGN, MOE_ALIGN), MOE_ALIGN)


def _k_chunks(k):
    step = k // MOE_KSPLIT
    return [slice(c * step, (c + 1) * step) for c in range(MOE_KSPLIT)]


def _expert_glu_kernel(ue_ref, uw_ref, ur_ref, x_ref, wg_ref, wl_ref, bg_ref, bl_ref, o_ref, wg_bf, wl_bf):
    rows = ur_ref[pl.program_id(0)]

    def cast():
        wg_bf[...] = wg_ref[...].astype(BF16)
        wl_bf[...] = wl_ref[...].astype(BF16)

    def finish(r0, size, gate, lin):
        gate = jnp.minimum(gate + bg_ref[...], SWIGLU_LIMIT)
        lin = jnp.clip(lin + bl_ref[...], -SWIGLU_LIMIT, SWIGLU_LIMIT)
        act = gate * jax.nn.sigmoid(SWIGLU_ALPHA * gate) * (lin + 1.0)
        o_ref[pl.ds(r0, size), :] = act.astype(o_ref.dtype)

    def block(r0, size):
        x = x_ref[pl.ds(r0, size), :]
        finish(r0, size, jnp.dot(x, wg_bf[...], preferred_element_type=F32),
               jnp.dot(x, wl_bf[...], preferred_element_type=F32))

    def fused_first():
        gate = lin = None
        for ks in _k_chunks(x_ref.shape[1]):
            wg_c = wg_ref[ks, :].astype(BF16)
            wl_c = wl_ref[ks, :].astype(BF16)
            wg_bf[ks, :] = wg_c
            wl_bf[ks, :] = wl_c
            x = x_ref[0:MOE_BIG, ks]
            pg = jnp.dot(x, wg_c, preferred_element_type=F32)
            pn = jnp.dot(x, wl_c, preferred_element_type=F32)
            gate = pg if gate is None else gate + pg
            lin = pn if lin is None else lin + pn
        finish(0, MOE_BIG, gate, lin)

    @pl.when(rows > 0)
    def _():
        _for_row_blocks(rows, block, cast, fused_first)


def _expert_down_kernel(ue_ref, uw_ref, ur_ref, x_ref, w_ref, b_ref, o_ref, w_bf):
    rows = ur_ref[pl.program_id(0)]

    def cast():
        w_bf[...] = w_ref[...].astype(BF16)

    def block(r0, size):
        x = x_ref[pl.ds(r0, size), :]
        o_ref[pl.ds(r0, size), :] = jnp.dot(x, w_bf[...], preferred_element_type=F32) + b_ref[...]

    def fused_first():
        acc = None
        for ks in _k_chunks(x_ref.shape[1]):
            w_c = w_ref[ks, :].astype(BF16)
            w_bf[ks, :] = w_c
            p = jnp.dot(x_ref[0:MOE_BIG, ks], w_c, preferred_element_type=F32)
            acc = p if acc is None else acc + p
        o_ref[0:MOE_BIG, :] = acc + b_ref[...]

    @pl.when(rows > 0)
    def _():
        _for_row_blocks(rows, block, cast, fused_first)


def _col(unit_rows, u, j, nt):
    return jnp.where(unit_rows[u] > 0, j, nt - 1)


def _out_block(unit_rows, u, j, n_units):
    used = unit_rows[u] > 0
    return jnp.where(used, u, n_units), jnp.where(used, j, 0)


def _expert_glu(x_sorted, w_gate_up, b_gate_up, unit_expert, unit_win, unit_rows, n_used):
    n_slots, d = x_sorted.shape
    n_e, _, two_h = w_gate_up.shape
    h = two_h // 2
    nt = h // MOE_TN
    n_units = unit_expert.shape[0]
    b3 = b_gate_up.reshape(n_e, 1, two_h)
    return pl.pallas_call(
        _expert_glu_kernel,
        grid_spec=pltpu.PrefetchScalarGridSpec(
            num_scalar_prefetch=3,
            grid=(n_used, nt),
            in_specs=[
                pl.BlockSpec((MOE_ROWS, d), lambda u, j, ue, uw, ur: (uw[u], 0)),
                pl.BlockSpec((None, d, MOE_TN), lambda u, j, ue, uw, ur: (ue[u], 0, _col(ur, u, j, nt))),
                pl.BlockSpec((None, d, MOE_TN), lambda u, j, ue, uw, ur: (ue[u], 0, nt + _col(ur, u, j, nt))),
                pl.BlockSpec((None, 1, MOE_TN), lambda u, j, ue, uw, ur: (ue[u], 0, _col(ur, u, j, nt))),
                pl.BlockSpec((None, 1, MOE_TN), lambda u, j, ue, uw, ur: (ue[u], 0, nt + _col(ur, u, j, nt))),
            ],
            out_specs=pl.BlockSpec((MOE_ROWS, MOE_TN), lambda u, j, ue, uw, ur: _out_block(ur, u, j, n_units)),
            scratch_shapes=[pltpu.VMEM((d, MOE_TN), BF16), pltpu.VMEM((d, MOE_TN), BF16)],
        ),
        out_shape=jax.ShapeDtypeStruct((n_slots + MOE_ROWS, h), BF16),
        compiler_params=_params("arbitrary", "arbitrary"),
        name="expert_glu",
    )(unit_expert, unit_win, unit_rows, x_sorted, w_gate_up, w_gate_up, b3, b3)


def _expert_down(act, w_down, b_down, unit_expert, unit_win, unit_rows, n_used):
    h = act.shape[1]
    n_e, _, d = w_down.shape
    n_units = unit_expert.shape[0]
    n_slots = (n_units + 1) * MOE_ROWS
    tn = MOE_TN_DOWN
    nt = d // tn
    b3 = b_down.reshape(n_e, 1, d)
    return pl.pallas_call(
        _expert_down_kernel,
        grid_spec=pltpu.PrefetchScalarGridSpec(
            num_scalar_prefetch=3,
            grid=(n_used, nt),
            in_specs=[
                pl.BlockSpec((MOE_ROWS, h), lambda u, j, ue, uw, ur: (uw[u], 0)),
                pl.BlockSpec((None, h, tn), lambda u, j, ue, uw, ur: (ue[u], 0, _col(ur, u, j, nt))),
                pl.BlockSpec((None, 1, tn), lambda u, j, ue, uw, ur: (ue[u], 0, _col(ur, u, j, nt))),
            ],
            out_specs=pl.BlockSpec((MOE_ROWS, tn), lambda u, j, ue, uw, ur: _out_block(ur, u, j, n_units)),
            scratch_shapes=[pltpu.VMEM((h, tn), BF16)],
        ),
        out_shape=jax.ShapeDtypeStruct((n_slots, d), F32),
        compiler_params=_params("arbitrary", "arbitrary"),
        name="expert_down",
    )(unit_expert, unit_win, unit_rows, act, w_down, b3)


def _combine_kernel(dest_ref, y_hbm, x_ref, r_ref, lg_ref, lb_ref, op_ref, os_ref, buf, sem,
                    *, tb, alpha, n_prompt_blocks):
    s = pl.program_id(0)
    n_steps = pl.num_programs(0)
    n_rows = tb * TOP_K

    def start(step, slot):
        base = step * n_rows

        def body(a, c):
            d = dest_ref[base + a]
            pltpu.make_async_copy(y_hbm.at[pl.ds(d, 1)], buf.at[slot, pl.ds(a, 1)], sem.at[slot]).start()
            return c
        lax.fori_loop(0, n_rows, body, 0, unroll=DMA_UNROLL)

    def wait(slot):
        pltpu.make_async_copy(y_hbm.at[pl.ds(0, n_rows)], buf.at[slot], sem.at[slot]).wait()

    @pl.when(s == 0)
    def _():
        start(0, 0)

    @pl.when(s + 1 < n_steps)
    def _():
        start(s + 1, (s + 1) % 2)

    slot = s % 2
    wait(slot)
    route = r_ref[...]
    f = jnp.zeros(x_ref.shape, F32)
    for k in range(TOP_K):
        f = f + buf[slot, k * tb:(k + 1) * tb] * route[:, ROUTE_GATE + k:ROUTE_GATE + k + 1]
    y = _ln(alpha * x_ref[...] + f, lg_ref[...], lb_ref[...])

    @pl.when(s < n_prompt_blocks)
    def _():
        op_ref[...] = y

    @pl.when(s >= n_prompt_blocks)
    def _():
        os_ref[...] = y


def _combine(dest, y_slots, x, route, ln_g, ln_b, alpha, tb, n_p):
    t, d = x.shape
    dest_flat = dest.reshape(t // tb, tb, TOP_K).transpose(0, 2, 1).reshape(-1)
    npb = n_p // tb
    prow, srow = _split_rows(npb)
    return pl.pallas_call(
        functools.partial(_combine_kernel, tb=tb, alpha=alpha, n_prompt_blocks=npb),
        grid_spec=pltpu.PrefetchScalarGridSpec(
            num_scalar_prefetch=1,
            grid=(t // tb,),
            in_specs=[
                pl.BlockSpec(memory_space=pl.ANY),
                pl.BlockSpec((tb, d), lambda s, dest: (s, 0)),
                pl.BlockSpec((tb, LANES), lambda s, dest: (s, 0)),
                pl.BlockSpec((1, d), lambda s, dest: (0, 0)),
                pl.BlockSpec((1, d), lambda s, dest: (0, 0)),
            ],
            out_specs=[
                pl.BlockSpec((tb, d), lambda s, dest: (prow(s), 0)),
                pl.BlockSpec((tb, d), lambda s, dest: (srow(s), 0)),
            ],
            scratch_shapes=[pltpu.VMEM((2, TOP_K * tb, d), F32), pltpu.SemaphoreType.DMA((2,))],
        ),
        out_shape=[
            jax.ShapeDtypeStruct((n_p, d), F32),
            jax.ShapeDtypeStruct((t - n_p, d), F32),
        ],
        compiler_params=_params("arbitrary"),
        name="combine_ln",
    )(dest_flat, y_slots, x, route, ln_g, ln_b)


def _rot_cols(w):
    half = ROPE_DIM // 2
    return jnp.concatenate([-w[..., half:], w[..., :half]], axis=-1)


def _rope_cs(pos):
    half = ROPE_DIM // 2
    inv_freq = ROPE_THETA ** (-jnp.arange(half, dtype=F32) / half)
    ang = pos.astype(F32)[:, None] * inv_freq[None, :]
    cos, sin = jnp.cos(ang), jnp.sin(ang)
    return jnp.concatenate([cos, cos, sin, sin], axis=-1)


def _kv_weight(w_uk, w_uv):
    wk = jnp.zeros((KV_IN, MLA_HEADS, QK_PAD), F32)
    wk = wk.at[:KV_LORA_RANK, :, :QK_NOPE_DIM].set(w_uk)
    eye = jnp.broadcast_to(jnp.eye(ROPE_DIM, dtype=F32)[:, None, :], (ROPE_DIM, MLA_HEADS, ROPE_DIM))
    wk = wk.at[KV_LORA_RANK:KV_LORA_RANK + ROPE_DIM, :, QK_NOPE_DIM:QK_NOPE_DIM + ROPE_DIM].set(eye)
    wv = jnp.zeros((KV_IN, MLA_HEADS, V_HEAD_DIM), F32).at[:KV_LORA_RANK].set(w_uv)
    return jnp.concatenate(
        [wk.reshape(KV_IN, MLA_HEADS * QK_PAD), wv.reshape(KV_IN, MLA_WIDTH)], axis=1).astype(BF16)


def _routing_tables(route, counts, n_tok):
    n_assign = n_tok * TOP_K
    n_units = n_assign // MOE_ROWS + N_EXPERTS
    n_slots = n_units * MOE_ROWS
    e = route[:, ROUTE_IDX:ROUTE_IDX + TOP_K].astype(jnp.int32)
    rank = route[:, ROUTE_RANK:ROUTE_RANK + TOP_K].astype(jnp.int32)
    units_per_e = (counts + MOE_ROWS - 1) // MOE_ROWS
    unit_end = jnp.cumsum(units_per_e)
    unit_start = unit_end - units_per_e
    total_units = unit_end[-1]
    dest = (unit_start[e] * MOE_ROWS + rank).reshape(-1)
    u = jnp.arange(n_units, dtype=jnp.int32)
    u_eff = jnp.minimum(u, jnp.maximum(total_units - 1, 0))
    unit_expert = jnp.minimum(jnp.searchsorted(unit_end, u_eff, side="right"), N_EXPERTS - 1).astype(jnp.int32)
    rows = jnp.clip(counts[unit_expert] - (u_eff - unit_start[unit_expert]) * MOE_ROWS, 0, MOE_ROWS)
    unit_rows = jnp.where(u < total_units, rows, 0).astype(jnp.int32)
    unit_win = u_eff.astype(jnp.int32)
    slot_tok = jnp.zeros((n_slots,), jnp.int32).at[dest].set(jnp.arange(n_assign, dtype=jnp.int32) // TOP_K)
    sub_per_unit = MOE_ROWS // MOE_SUB
    n_steps = n_assign // MOE_SUB + N_EXPERTS
    blk_per_unit = (unit_rows + MOE_SUB - 1) // MOE_SUB
    blk_end = jnp.cumsum(blk_per_unit)
    n_active = blk_end[-1]
    st = jnp.minimum(jnp.arange(n_steps, dtype=jnp.int32), jnp.maximum(n_active - 1, 0))
    st_unit = jnp.minimum(jnp.searchsorted(blk_end, st, side="right"), n_units - 1).astype(jnp.int32)
    act_blk = (st_unit * sub_per_unit + st - (blk_end - blk_per_unit)[st_unit]).astype(jnp.int32)
    return dest.astype(jnp.int32), slot_tok, act_blk, n_active.reshape(1).astype(jnp.int32), \
        unit_expert, unit_win, unit_rows, total_units.astype(jnp.int32), n_slots


def _pick_block(n, target, mult):
    best = None
    for b in range(mult, min(n, target) + 1, mult):
        if n % b == 0:
            best = b
    assert best is not None, (n, target, mult)
    return best


def kernel(x_prompt, x_sample, cache_kv_latent, cache_k_rope, w_in, q_a_norm_g, kv_a_norm_g, w_q_b, w_uk, w_uv, gmlp_norm_g, gmlp_norm_b, gmlp_w_s, gmlp_b_s, w_o, ln1_g, ln1_b, w_router, b_router, w_gate_up, b_gate_up, w_down, b_down, ln2_g, ln2_b):
    batch, seq, d_model = x_prompt.shape
    dbatch, dseq, _ = x_sample.shape
    depth, _, past, _ = cache_kv_latent.shape
    n_p, n_s = batch * seq, dbatch * dseq
    n_tok = n_p + n_s
    alpha = (2 * depth) ** 0.25
    assert n_p % GMLP_CHUNK == 0 and n_s % GMLP_CHUNK == 0 and GMLP_CHUNK % dseq == 0

    bm_big = _pick_block(n_tok, 1088, 16)
    bm_mid = _pick_block(n_tok, 512, 16)
    bm_split = _pick_block(math.gcd(n_p, n_s), 256, 16)
    tb = _pick_block(math.gcd(n_p, n_s), 128, 8)

    pos = jnp.concatenate([
        jnp.tile(jnp.arange(seq, dtype=jnp.int32), batch),
        jnp.tile(past + jnp.arange(dseq, dtype=jnp.int32), dbatch)])
    kcs = _rope_cs(pos)
    qtab = jnp.concatenate([jnp.ones((n_tok, QK_NOPE_DIM), F32), kcs], axis=-1) * ATTN_SCALE

    x_p = x_prompt.reshape(n_p, d_model)
    x_s = x_sample.reshape(n_s, d_model)

    lat_p, kr_p_all, lat_s, kr_s_all, gv_s_all = [], [], [], [], []
    for l in range(depth):
        w_kr = w_in[l][:, COL_KV:COL_KR]
        w_lat = jnp.concatenate([w_in[l][:, :COL_KR], _rot_cols(w_kr)], axis=1).astype(BF16)
        w_zu = w_in[l][:, COL_KR:COL_KR + GMLP_WIDTH].astype(BF16)
        w_zv = w_in[l][:, COL_KR + GMLP_WIDTH:].astype(BF16)
        wq = w_q_b[l]
        w_q = jnp.concatenate(
            [wq, _rot_cols(wq[..., QK_NOPE_DIM:])], axis=-1).reshape(Q_LORA_RANK, MLA_HEADS * QK_PAD).astype(BF16)
        w_kv = _kv_weight(w_uk[l], w_uv[l])
        cpos = jnp.arange(GMLP_CHUNK)
        cmask = (cpos[None, :] // CHUNK) <= (cpos[:, None] // CHUNK)
        w_sp_p = jnp.where(cmask[None], gmlp_w_s[l], 0.0)
        reps = GMLP_CHUNK // dseq
        dpos = jnp.arange(dseq)
        dmask = (dpos[None, :] // CHUNK) <= (dpos[:, None] // CHUNK)
        w_d = jnp.where(dmask[None], gmlp_w_s[l][:, :dseq, :dseq], 0.0)
        w_sp_s = jnp.einsum("ab,gij->gaibj", jnp.eye(reps, dtype=F32), w_d).reshape(
            GMLP_GROUPS, GMLP_CHUNK, GMLP_CHUNK)
        w_sets = jnp.stack([w_sp_p, w_sp_s]).astype(BF16)
        b_sets = jnp.stack([gmlp_b_s[l].T, jnp.tile(gmlp_b_s[l][:, :dseq], (1, reps)).T])
        w_o_bf = w_o[l].astype(BF16)
        w_r = jnp.zeros((d_model, LANES), F32).at[:, :N_EXPERTS].set(w_router[l])
        b_r = jnp.full((1, LANES), -jnp.inf, F32).at[0, :N_EXPERTS].set(b_router[l])

        cq, ckv, kpe, a_new, x_bf = _inproj_lat(
            x_p, x_s, w_lat, q_a_norm_g[l][None], kv_a_norm_g[l][None], kcs, bm_split)
        zu = _inproj_gelu(x_bf, w_zu, BF16, bm_big, 512)
        zv = _inproj_gelu(x_bf, w_zv, F32, bm_big, 512)
        q = _qproj(cq, w_q, qtab, bm_big)

        kv_p = _matmul(a_new, w_kv, BF16, bm_big, 1024, "kv_prompt")
        tq = min(512, seq)
        att_p = _attention(q, 0, kv_p, batch, seq, seq, seq, 0, tq, tq, 4)

        sk_valid = past + dseq
        sk = (sk_valid + LANES - 1) // LANES * LANES
        cache_a = jnp.concatenate([
            cache_kv_latent[l], cache_k_rope[l],
            jnp.zeros((dbatch, past, KV_IN - KV_LORA_RANK - ROPE_DIM), F32)], axis=-1).astype(BF16)
        a_s = jnp.concatenate([
            cache_a, a_new[n_p:].reshape(dbatch, dseq, KV_IN),
            jnp.zeros((dbatch, sk - sk_valid, KV_IN), BF16)], axis=1).reshape(dbatch * sk, KV_IN)
        wuk_t = jnp.transpose(w_uk[l], (1, 2, 0)).astype(BF16)
        wuv_h = jnp.transpose(w_uv[l], (1, 0, 2)).astype(BF16)
        att_s = _attention_latent(q, n_p, a_s, wuk_t, wuv_h, dbatch, dseq, sk, sk_valid, past)
        att = jnp.concatenate([att_p, att_s], axis=0)

        g, vn = _gmlp(zu, zv, gmlp_norm_g[l][None], gmlp_norm_b[l][None], w_sets, b_sets, n_p // GMLP_CHUNK)
        x1 = _oproj(att, g, w_o_bf[:MLA_WIDTH], w_o_bf[MLA_WIDTH:], x_p, x_s, ln1_g[l][None], ln1_b[l][None],
                    alpha, bm_split, 1024)

        route, cnt = _router(x1, w_r, b_r, bm_mid)
        counts = cnt[0, :N_EXPERTS].astype(jnp.int32)
        dest, slot_tok, act_blk, n_active, unit_expert, unit_win, unit_rows, n_used, n_slots = _routing_tables(
            route, counts, n_tok)
        x_sorted = _dispatch(x1, slot_tok, act_blk, n_active, n_slots, MOE_SUB)
        act = _expert_glu(x_sorted, w_gate_up[l], b_gate_up[l], unit_expert, unit_win, unit_rows, n_used)
        y_slots = _expert_down(act, w_down[l], b_down[l], unit_expert, unit_win, unit_rows, n_used)
        x_p, x_s = _combine(dest, y_slots, x1, route, ln2_g[l][None], ln2_b[l][None], alpha, tb, n_p)

        lat_p.append(ckv[:n_p].reshape(batch, seq, KV_LORA_RANK))
        kr_p_all.append(kpe[:n_p].reshape(batch, seq, ROPE_DIM))
        lat_s.append(ckv[n_p:].reshape(dbatch, dseq, KV_LORA_RANK))
        kr_s_all.append(kpe[n_p:].reshape(dbatch, dseq, ROPE_DIM))
        gv_s_all.append(vn[n_p:].reshape(dbatch, dseq, GMLP_GROUPS, GMLP_GROUP_DIM))

    return (
        x_p.reshape(batch, seq, d_model),
        x_s.reshape(dbatch, dseq, d_model),
        jnp.stack(lat_p), jnp.stack(kr_p_all), jnp.stack(lat_s), jnp.stack(kr_s_all), jnp.stack(gv_s_all),
    )
```

```python
import functools
import math

import jax
import jax.numpy as jnp
from jax import lax
from jax.experimental import pallas as pl
from jax.experimental.pallas import tpu as pltpu

F32 = jnp.float32
BF16 = jnp.bfloat16

CHUNK = 64
MLA_HEADS = 16
QK_NOPE_DIM = 128
ROPE_DIM = 64
V_HEAD_DIM = 128
Q_LORA_RANK = 768
KV_LORA_RANK = 512
ROPE_THETA = 10000.0
ATTN_SCALE = (QK_NOPE_DIM + ROPE_DIM) ** -0.5
GMLP_GROUPS = 16
GMLP_GROUP_DIM = 128
GMLP_WIDTH = GMLP_GROUPS * GMLP_GROUP_DIM
GMLP_CHUNK = 128
MLA_WIDTH = MLA_HEADS * V_HEAD_DIM
COL_Q = Q_LORA_RANK
COL_KV = COL_Q + KV_LORA_RANK
COL_KR = COL_KV + ROPE_DIM
N_EXPERTS = 32
TOP_K = 4
SWIGLU_ALPHA = 1.702
SWIGLU_LIMIT = 7.0
LN_EPS = 1e-5
RMS_EPS = 1e-6

LANES = 128
QK_PAD = 256
KV_IN = KV_LORA_RANK + LANES
KV_OUT = MLA_HEADS * QK_PAD + MLA_HEADS * V_HEAD_DIM
MOE_ROWS = 1280
MOE_SUB = 256
MOE_BIG = 1024
MOE_ALIGN = 128
MOE_KSPLIT = 4
MOE_TN = 256
MOE_TN_DOWN = 512
DMA_UNROLL = 8
VMEM_LIMIT = 56 * 1024 * 1024


def _params(*sem):
    return pltpu.CompilerParams(dimension_semantics=sem, vmem_limit_bytes=VMEM_LIMIT)


def _rms(x, g):
    return x * lax.rsqrt(jnp.mean(x * x, axis=-1, keepdims=True) + RMS_EPS) * g


def _ln(x, g, b):
    mu = jnp.mean(x, axis=-1, keepdims=True)
    d = x - mu
    var = jnp.mean(d * d, axis=-1, keepdims=True)
    return d * lax.rsqrt(var + LN_EPS) * g + b


def _inproj_lat_kernel(xp_ref, xs_ref, w_ref, gq_ref, gkv_ref, cs_ref, cq_ref, ckv_ref, kpe_ref, a_ref, xbf_ref,
                       *, n_prompt_blocks):
    i = pl.program_id(0)

    @pl.when(i < n_prompt_blocks)
    def _():
        xbf_ref[...] = xp_ref[...].astype(BF16)

    @pl.when(i >= n_prompt_blocks)
    def _():
        xbf_ref[...] = xs_ref[...].astype(BF16)

    p = jnp.dot(xbf_ref[...], w_ref[...], preferred_element_type=F32)
    cq_ref[...] = _rms(p[:, :COL_Q], gq_ref[...]).astype(BF16)
    ckv = _rms(p[:, COL_Q:COL_KV], gkv_ref[...])
    ckv_ref[...] = ckv
    t = p[:, COL_KV:] * cs_ref[...]
    r = t + pltpu.roll(t, ROPE_DIM, axis=1)
    kpe_ref[...] = r[:, :ROPE_DIM]
    a_ref[:, :KV_LORA_RANK] = ckv.astype(BF16)
    lane = lax.broadcasted_iota(jnp.int32, r.shape, 1)
    a_ref[:, KV_LORA_RANK:] = jnp.where(lane < ROPE_DIM, r, 0.0).astype(BF16)


def _split_rows(n_prompt_blocks):
    prompt = lambda i: jnp.minimum(i, n_prompt_blocks - 1)
    sample = lambda i: jnp.maximum(i - n_prompt_blocks, 0)
    return prompt, sample


def _inproj_lat(x_p, x_s, w_lat, gq, gkv, kcs, bm):
    (n_p, d), n_s = x_p.shape, x_s.shape[0]
    t = n_p + n_s
    n = w_lat.shape[1]
    npb = n_p // bm
    prow, srow = _split_rows(npb)
    return pl.pallas_call(
        functools.partial(_inproj_lat_kernel, n_prompt_blocks=npb),
        grid=(t // bm,),
        in_specs=[
            pl.BlockSpec((bm, d), lambda i: (prow(i), 0)),
            pl.BlockSpec((bm, d), lambda i: (srow(i), 0)),
            pl.BlockSpec((d, n), lambda i: (0, 0)),
            pl.BlockSpec((1, COL_Q), lambda i: (0, 0)),
            pl.BlockSpec((1, KV_LORA_RANK), lambda i: (0, 0)),
            pl.BlockSpec((bm, LANES), lambda i: (i, 0)),
        ],
        out_specs=[
            pl.BlockSpec((bm, COL_Q), lambda i: (i, 0)),
            pl.BlockSpec((bm, KV_LORA_RANK), lambda i: (i, 0)),
            pl.BlockSpec((bm, ROPE_DIM), lambda i: (i, 0)),
            pl.BlockSpec((bm, KV_IN), lambda i: (i, 0)),
            pl.BlockSpec((bm, d), lambda i: (i, 0)),
        ],
        out_shape=[
            jax.ShapeDtypeStruct((t, COL_Q), BF16),
            jax.ShapeDtypeStruct((t, KV_LORA_RANK), F32),
            jax.ShapeDtypeStruct((t, ROPE_DIM), F32),
            jax.ShapeDtypeStruct((t, KV_IN), BF16),
            jax.ShapeDtypeStruct((t, d), BF16),
        ],
        compiler_params=_params("parallel"),
        name="inproj_lat",
    )(x_p, x_s, w_lat, gq, gkv, kcs)


def _inproj_gelu_kernel(x_ref, w_ref, o_ref):
    p = jnp.dot(x_ref[...], w_ref[...], preferred_element_type=F32)
    o_ref[...] = (0.5 * p * (1.0 + lax.erf(p * (2.0 ** -0.5)))).astype(o_ref.dtype)


def _inproj_gelu(x_bf, w, out_dtype, bm, bn):
    t, d = x_bf.shape
    n = w.shape[1]
    return pl.pallas_call(
        _inproj_gelu_kernel,
        grid=(t // bm, n // bn),
        in_specs=[
            pl.BlockSpec((bm, d), lambda i, j: (i, 0)),
            pl.BlockSpec((d, bn), lambda i, j: (0, j)),
        ],
        out_specs=pl.BlockSpec((bm, bn), lambda i, j: (i, j)),
        out_shape=jax.ShapeDtypeStruct((t, n), out_dtype),
        compiler_params=_params("parallel", "parallel"),
        name="inproj_gelu",
    )(x_bf, w)


def _qproj_kernel(c_ref, w_ref, tab_ref, o_ref, *, heads):
    c = c_ref[...]
    tab = tab_ref[...]
    for h in range(heads):
        c0 = h * QK_PAD
        p = jnp.dot(c, w_ref[:, c0:c0 + QK_PAD], preferred_element_type=F32) * tab
        o_ref[:, c0:c0 + QK_NOPE_DIM] = p[:, :QK_NOPE_DIM].astype(BF16)
        t = p[:, QK_NOPE_DIM:]
        o_ref[:, c0 + QK_NOPE_DIM:c0 + QK_PAD] = (t + pltpu.roll(t, ROPE_DIM, axis=1)).astype(BF16)


def _qproj(cq, w_q, qtab, bm, heads=4):
    t, r = cq.shape
    return pl.pallas_call(
        functools.partial(_qproj_kernel, heads=heads),
        grid=(t // bm, MLA_HEADS // heads),
        in_specs=[
            pl.BlockSpec((bm, r), lambda i, h: (i, 0)),
            pl.BlockSpec((r, heads * QK_PAD), lambda i, h: (0, h)),
            pl.BlockSpec((bm, QK_PAD), lambda i, h: (i, 0)),
        ],
        out_specs=pl.BlockSpec((bm, heads * QK_PAD), lambda i, h: (i, h)),
        out_shape=jax.ShapeDtypeStruct((t, MLA_HEADS * QK_PAD), BF16),
        compiler_params=_params("parallel", "parallel"),
        name="qproj",
    )(cq, w_q, qtab)


def _matmul_kernel(x_ref, w_ref, o_ref):
    o_ref[...] = jnp.dot(x_ref[...], w_ref[...], preferred_element_type=F32).astype(o_ref.dtype)


def _matmul(x, w, out_dtype, bm, bn, name):
    m, k = x.shape
    n = w.shape[1]
    return pl.pallas_call(
        _matmul_kernel,
        grid=(m // bm, n // bn),
        in_specs=[
            pl.BlockSpec((bm, k), lambda i, j: (i, 0)),
            pl.BlockSpec((k, bn), lambda i, j: (0, j)),
        ],
        out_specs=pl.BlockSpec((bm, bn), lambda i, j: (i, j)),
        out_shape=jax.ShapeDtypeStruct((m, n), out_dtype),
        compiler_params=_params("parallel", "parallel"),
        name=name,
    )(x, w)


def _attn_kernel(q_ref, k_ref, v_ref, o_ref, *, tq, tk, sk_valid, q_pos0, heads):
    i = pl.program_id(2)
    q_lo = q_pos0 + i * tq
    seen_by_all = jnp.minimum((q_lo // CHUNK + 1) * CHUNK, sk_valid)
    seen_by_any = jnp.minimum(((q_lo + tq - 1) // CHUNK + 1) * CHUNK, sk_valid)
    n_full = seen_by_all // tk
    n_all = (seen_by_any + tk - 1) // tk

    def step(masked):
        def body(j, carry):
            k0 = pl.multiple_of(j * tk, tk)
            if masked:
                qpos = q_lo + lax.broadcasted_iota(jnp.int32, (tq, tk), 0)
                kpos = k0 + lax.broadcasted_iota(jnp.int32, (tq, tk), 1)
                ok = (kpos // CHUNK <= qpos // CHUNK) & (kpos < sk_valid)
            out = []
            for h, (m, l, acc) in enumerate(carry):
                q = q_ref[:, h * QK_PAD:(h + 1) * QK_PAD]
                k = k_ref[pl.ds(k0, tk), h * QK_PAD:(h + 1) * QK_PAD]
                s = lax.dot_general(q, k, (((1,), (1,)), ((), ())), preferred_element_type=F32)
                if masked:
                    s = jnp.where(ok, s, -jnp.inf)
                m_new = jnp.maximum(m, jnp.max(s, axis=-1, keepdims=True))
                a = jnp.exp(m - m_new)
                p = jnp.exp(s - m_new)
                l = a * l + jnp.sum(p, axis=-1, keepdims=True)
                v = v_ref[pl.ds(k0, tk), h * V_HEAD_DIM:(h + 1) * V_HEAD_DIM]
                acc = a * acc + jnp.dot(p.astype(BF16), v, preferred_element_type=F32)
                out.append((m_new, l, acc))
            return tuple(out)

        return body

    init = tuple(
        (jnp.full((tq, 1), -1e30, F32), jnp.zeros((tq, 1), F32), jnp.zeros((tq, V_HEAD_DIM), F32))
        for _ in range(heads))
    carry = lax.fori_loop(0, n_full, step(False), init)
    carry = lax.fori_loop(n_full, n_all, step(True), carry)
    for h, (_, l, acc) in enumerate(carry):
        o_ref[:, h * V_HEAD_DIM:(h + 1) * V_HEAD_DIM] = (acc / l).astype(BF16)


def _attention(q, q_row0, kv, batch, sq, sk, sk_valid, q_pos0, tq, tk, heads):
    nq = sq // tq
    qb0 = q_row0 // tq
    v_col0 = MLA_HEADS * QK_PAD // (heads * V_HEAD_DIM)
    return pl.pallas_call(
        functools.partial(_attn_kernel, tq=tq, tk=tk, sk_valid=sk_valid, q_pos0=q_pos0, heads=heads),
        grid=(batch, MLA_HEADS // heads, nq),
        in_specs=[
            pl.BlockSpec((tq, heads * QK_PAD), lambda b, h, i: (qb0 + b * nq + i, h)),
            pl.BlockSpec((sk, heads * QK_PAD), lambda b, h, i: (b, h)),
            pl.BlockSpec((sk, heads * V_HEAD_DIM), lambda b, h, i: (b, v_col0 + h)),
        ],
        out_specs=pl.BlockSpec((tq, heads * V_HEAD_DIM), lambda b, h, i: (b * nq + i, h)),
        out_shape=jax.ShapeDtypeStruct((batch * sq, MLA_WIDTH), BF16),
        compiler_params=_params("parallel", "parallel", "parallel"),
        name="attention",
    )(q, kv, kv)


def _attn_latent_kernel(q_ref, a_ref, wuk_ref, wuv_ref, o_ref, qc_ref, *, sq, sk_valid, q_pos0):
    for h in range(MLA_HEADS):
        rows = slice(h * sq, (h + 1) * sq)
        c0 = h * QK_PAD
        q_lat = jnp.dot(q_ref[:, c0:c0 + QK_NOPE_DIM], wuk_ref[h], preferred_element_type=F32)
        qc_ref[rows, :KV_LORA_RANK] = q_lat.astype(BF16)
        qc_ref[rows, KV_LORA_RANK:] = q_ref[:, c0 + QK_NOPE_DIM:c0 + QK_PAD]
    a = a_ref[...]
    s = lax.dot_general(qc_ref[...], a, (((1,), (1,)), ((), ())), preferred_element_type=F32)
    n_rows, sk = s.shape
    qpos = q_pos0 + lax.broadcasted_iota(jnp.int32, (n_rows, sk), 0) % sq
    kpos = lax.broadcasted_iota(jnp.int32, (n_rows, sk), 1)
    ok = (kpos // CHUNK <= qpos // CHUNK) & (kpos < sk_valid)
    s = jnp.where(ok, s, -jnp.inf)
    p = jnp.exp(s - jnp.max(s, axis=-1, keepdims=True))
    l = jnp.sum(p, axis=-1, keepdims=True)
    o_lat = jnp.dot(p.astype(BF16), a[:, :KV_LORA_RANK], preferred_element_type=F32) / l
    for h in range(MLA_HEADS):
        o = jnp.dot(o_lat[h * sq:(h + 1) * sq].astype(BF16), wuv_ref[h], preferred_element_type=F32)
        o_ref[:, h * V_HEAD_DIM:(h + 1) * V_HEAD_DIM] = o.astype(BF16)


def _attention_latent(q, q_row0, a, wuk_t, wuv, batch, sq, sk, sk_valid, q_pos0):
    qb0 = q_row0 // sq
    return pl.pallas_call(
        functools.partial(_attn_latent_kernel, sq=sq, sk_valid=sk_valid, q_pos0=q_pos0),
        grid=(batch,),
        in_specs=[
            pl.BlockSpec((sq, MLA_HEADS * QK_PAD), lambda b: (qb0 + b, 0)),
            pl.BlockSpec((sk, KV_IN), lambda b: (b, 0)),
            pl.BlockSpec((MLA_HEADS, QK_NOPE_DIM, KV_LORA_RANK), lambda b: (0, 0, 0)),
            pl.BlockSpec((MLA_HEADS, KV_LORA_RANK, V_HEAD_DIM), lambda b: (0, 0, 0)),
        ],
        out_specs=pl.BlockSpec((sq, MLA_WIDTH), lambda b: (b, 0)),
        out_shape=jax.ShapeDtypeStruct((batch * sq, MLA_WIDTH), BF16),
        scratch_shapes=[pltpu.VMEM((MLA_HEADS * sq, KV_IN), BF16)],
        compiler_params=_params("parallel"),
        name="attention_latent",
    )(q, a, wuk_t, wuv)


def _gmlp_kernel(u_ref, v_ref, g_ref, b_ref, w_ref, bs_ref, o_ref, vn_ref):
    vn = _ln(v_ref[...], g_ref[...], b_ref[...])
    vn_ref[...] = vn
    bs = bs_ref[...]
    for g in range(GMLP_GROUPS):
        cols = slice(g * GMLP_GROUP_DIM, (g + 1) * GMLP_GROUP_DIM)
        mix = jnp.dot(w_ref[g], vn[:, cols].astype(BF16), preferred_element_type=F32)
        mix = mix + bs[:, g:g + 1]
        o_ref[:, cols] = (u_ref[:, cols].astype(F32) * mix).astype(BF16)


def _gmlp(zu, zv, g, b, w_sets, b_sets, n_prompt_chunks):
    t, n = zu.shape
    c = GMLP_CHUNK
    pick = lambda i: jnp.where(i >= n_prompt_chunks, 1, 0)
    return pl.pallas_call(
        _gmlp_kernel,
        grid=(t // c,),
        in_specs=[
            pl.BlockSpec((c, n), lambda i: (i, 0)),
            pl.BlockSpec((c, n), lambda i: (i, 0)),
            pl.BlockSpec((1, n), lambda i: (0, 0)),
            pl.BlockSpec((1, n), lambda i: (0, 0)),
            pl.BlockSpec((None, GMLP_GROUPS, c, c), lambda i: (pick(i), 0, 0, 0)),
            pl.BlockSpec((None, c, GMLP_GROUPS), lambda i: (pick(i), 0, 0)),
        ],
        out_specs=[
            pl.BlockSpec((c, n), lambda i: (i, 0)),
            pl.BlockSpec((c, n), lambda i: (i, 0)),
        ],
        out_shape=[
            jax.ShapeDtypeStruct((t, n), BF16),
            jax.ShapeDtypeStruct((t, n), F32),
        ],
        compiler_params=_params("parallel"),
        name="gmlp",
    )(zu, zv, g, b, w_sets, b_sets)


def _oproj_kernel(att_ref, g_ref, wa_ref, wg_ref, xp_ref, xs_ref, lg_ref, lb_ref, o_ref, acc_ref,
                  *, bn, alpha, n_prompt_blocks):
    i = pl.program_id(0)
    j = pl.program_id(1)
    nj, bm, _ = acc_ref.shape
    acc_ref[j] = (
        jnp.dot(att_ref[...], wa_ref[...], preferred_element_type=F32)
        + jnp.dot(g_ref[...], wg_ref[...], preferred_element_type=F32)
    )

    def finish(x_ref):
        d = nj * bn
        s1 = jnp.zeros((bm, 1), F32)
        for c in range(nj):
            y = alpha * x_ref[:, c * bn:(c + 1) * bn] + acc_ref[c]
            acc_ref[c] = y
            s1 = s1 + jnp.sum(y, axis=-1, keepdims=True)
        mu = s1 / d
        s2 = jnp.zeros((bm, 1), F32)
        for c in range(nj):
            dev = acc_ref[c] - mu
            s2 = s2 + jnp.sum(dev * dev, axis=-1, keepdims=True)
        inv = lax.rsqrt(s2 / d + LN_EPS)
        for c in range(nj):
            cols = slice(c * bn, (c + 1) * bn)
            o_ref[:, cols] = (acc_ref[c] - mu) * inv * lg_ref[:, cols] + lb_ref[:, cols]

    @pl.when((j == nj - 1) & (i < n_prompt_blocks))
    def _():
        finish(xp_ref)

    @pl.when((j == nj - 1) & (i >= n_prompt_blocks))
    def _():
        finish(xs_ref)


def _oproj(att, g, w_att, w_g, x_p, x_s, ln_g, ln_b, alpha, bm, bn):
    t, ka = att.shape
    kg = g.shape[1]
    n_p, d = x_p.shape
    npb = n_p // bm
    prow, srow = _split_rows(npb)
    return pl.pallas_call(
        functools.partial(_oproj_kernel, bn=bn, alpha=alpha, n_prompt_blocks=npb),
        grid=(t // bm, d // bn),
        in_specs=[
            pl.BlockSpec((bm, ka), lambda i, j: (i, 0)),
            pl.BlockSpec((bm, kg), lambda i, j: (i, 0)),
            pl.BlockSpec((ka, bn), lambda i, j: (0, j)),
            pl.BlockSpec((kg, bn), lambda i, j: (0, j)),
            pl.BlockSpec((bm, d), lambda i, j: (prow(i), 0)),
            pl.BlockSpec((bm, d), lambda i, j: (srow(i), 0)),
            pl.BlockSpec((1, d), lambda i, j: (0, 0)),
            pl.BlockSpec((1, d), lambda i, j: (0, 0)),
        ],
        out_specs=pl.BlockSpec((bm, d), lambda i, j: (i, 0)),
        out_shape=jax.ShapeDtypeStruct((t, d), F32),
        scratch_shapes=[pltpu.VMEM((d // bn, bm, bn), F32)],
        compiler_params=_params("parallel", "arbitrary"),
        name="oproj_ln",
    )(att, g, w_att, w_g, x_p, x_s, ln_g, ln_b)


ROUTE_IDX, ROUTE_GATE, ROUTE_RANK = 0, TOP_K, 2 * TOP_K


def _router_kernel(x_ref, w_ref, b_ref, o_ref, cnt_out_ref, cnt_ref, *, bm):
    @pl.when(pl.program_id(0) == 0)
    def _():
        cnt_ref[...] = jnp.zeros_like(cnt_ref)

    logits = jnp.dot(x_ref[...], w_ref[...], preferred_element_type=F32,
                     precision=lax.Precision.HIGHEST) + b_ref[...]
    lane = lax.broadcasted_iota(jnp.int32, (bm, LANES), 1)
    work = logits
    vals, hits = [], []
    out = jnp.zeros((bm, LANES), F32)
    for k in range(TOP_K):
        m = jnp.max(work, axis=-1, keepdims=True)
        idx = jnp.min(jnp.where(work == m, lane, LANES), axis=-1, keepdims=True)
        hit = lane == idx
        work = jnp.where(hit, -jnp.inf, work)
        vals.append(m)
        hits.append(hit)
        out = jnp.where(lane == ROUTE_IDX + k, idx.astype(F32), out)
    exps = [jnp.exp(v - vals[0]) for v in vals]
    den = exps[0] + exps[1] + exps[2] + exps[3]
    member = jnp.zeros((bm, LANES), F32)
    for hit in hits:
        member = member + hit.astype(F32)
    row = lax.broadcasted_iota(jnp.int32, (bm, bm), 0)
    col = lax.broadcasted_iota(jnp.int32, (bm, bm), 1)
    before = (col < row).astype(BF16)
    prefix = jnp.dot(before, member.astype(BF16), preferred_element_type=F32) + cnt_ref[0:1, :]
    for k in range(TOP_K):
        out = jnp.where(lane == ROUTE_GATE + k, exps[k] / den, out)
        rank = jnp.sum(jnp.where(hits[k], prefix, 0.0), axis=-1, keepdims=True)
        out = jnp.where(lane == ROUTE_RANK + k, rank, out)
    o_ref[...] = out
    cnt_ref[0:1, :] = cnt_ref[0:1, :] + jnp.sum(member, axis=0, keepdims=True)
    cnt_out_ref[...] = cnt_ref[...]


def _router(x, w_pad, b_pad, bm):
    t, d = x.shape
    return pl.pallas_call(
        functools.partial(_router_kernel, bm=bm),
        grid=(t // bm,),
        in_specs=[
            pl.BlockSpec((bm, d), lambda i: (i, 0)),
            pl.BlockSpec((d, LANES), lambda i: (0, 0)),
            pl.BlockSpec((1, LANES), lambda i: (0, 0)),
        ],
        out_specs=[
            pl.BlockSpec((bm, LANES), lambda i: (i, 0)),
            pl.BlockSpec((8, LANES), lambda i: (0, 0)),
        ],
        out_shape=[
            jax.ShapeDtypeStruct((t, LANES), F32),
            jax.ShapeDtypeStruct((8, LANES), F32),
        ],
        scratch_shapes=[pltpu.VMEM((8, LANES), F32)],
        compiler_params=_params("arbitrary"),
        name="router",
    )(x, w_pad, b_pad)


def _dispatch_kernel(blk_ref, nact_ref, tok_ref, x_hbm, o_ref, buf, sem, *, rows):
    s = pl.program_id(0)
    n_active = nact_ref[0]

    def start(step, slot):
        base = blk_ref[step] * rows

        def body(r, c):
            tok = tok_ref[base + r]
            pltpu.make_async_copy(x_hbm.at[pl.ds(tok, 1)], buf.at[slot, pl.ds(r, 1)], sem.at[slot]).start()
            return c
        lax.fori_loop(0, rows, body, 0, unroll=DMA_UNROLL)

    def wait(step, slot):
        pltpu.make_async_copy(x_hbm.at[pl.ds(0, rows)], buf.at[slot], sem.at[slot]).wait()

    @pl.when((s == 0) & (n_active > 0))
    def _():
        start(0, 0)

    @pl.when(s + 1 < n_active)
    def _():
        start(s + 1, (s + 1) % 2)

    @pl.when(s < n_active)
    def _():
        wait(s, s % 2)
        o_ref[...] = buf[s % 2].astype(BF16)


def _dispatch(x, slot_tok, act_blk, n_active, n_slots, rows):
    t, d = x.shape
    n_steps = act_blk.shape[0]
    return pl.pallas_call(
        functools.partial(_dispatch_kernel, rows=rows),
        grid_spec=pltpu.PrefetchScalarGridSpec(
            num_scalar_prefetch=3,
            grid=(n_steps,),
            in_specs=[pl.BlockSpec(memory_space=pl.ANY)],
            out_specs=pl.BlockSpec((rows, d), lambda s, blk, nact, tok: (blk[s], 0)),
            scratch_shapes=[pltpu.VMEM((2, rows, d), F32), pltpu.SemaphoreType.DMA((2,))],
        ),
        out_shape=jax.ShapeDtypeStruct((n_slots, d), BF16),
        compiler_params=_params("arbitrary"),
        name="dispatch",
    )(act_blk, n_active, slot_tok, x)


def _for_row_blocks(rows, fn, cast, fused_first):
    padded = (rows + MOE_ALIGN - 1) // MOE_ALIGN * MOE_ALIGN
    n_big = padded // MOE_BIG

    @pl.when(n_big > 0)
    def _():
        fused_first()

    @pl.when(n_big == 0)
    def _():
        cast()

    def body(s, c):
        fn(pl.multiple_of(s * MOE_BIG, MOE_BIG), MOE_BIG)
        return c

    lax.fori_loop(1, n_big, body, 0)
    base = n_big * MOE_BIG
    rem = padded - base

    size = MOE_BIG // 2
    while size >= MOE_ALIGN:
        def tail(size=size):
            fn(pl.multiple_of(base + rem // (2 * size) * (2 * size), size), size)

        pl.when(rem // size % 2 == 1)(tail)
        size //= 2


def _k_chunks(k):
    step = k // MOE_KSPLIT
    return [slice(c * step, (c + 1) * step) for c in range(MOE_KSPLIT)]


def _expert_glu_kernel(ue_ref, uw_ref, ur_ref, x_ref, wg_ref, wl_ref, bg_ref, bl_ref, o_ref, wg_bf, wl_bf):
    rows = ur_ref[pl.program_id(0)]

    def cast():
        wg_bf[...] = wg_ref[...].astype(BF16)
        wl_bf[...] = wl_ref[...].astype(BF16)

    def finish(r0, size, gate, lin):
        gate = jnp.minimum(gate + bg_ref[...], SWIGLU_LIMIT)
        lin = jnp.clip(lin + bl_ref[...], -SWIGLU_LIMIT, SWIGLU_LIMIT)
        act = gate * jax.nn.sigmoid(SWIGLU_ALPHA * gate) * (lin + 1.0)
        o_ref[pl.ds(r0, size), :] = act.astype(o_ref.dtype)

    def block(r0, size):
        x = x_ref[pl.ds(r0, size), :]
        finish(r0, size, jnp.dot(x, wg_bf[...], preferred_element_type=F32),
               jnp.dot(x, wl_bf[...], preferred_element_type=F32))

    def fused_first():
        gate = lin = None
        for ks in _k_chunks(x_ref.shape[1]):
            wg_c = wg_ref[ks, :].astype(BF16)
            wl_c = wl_ref[ks, :].astype(BF16)
            wg_bf[ks, :] = wg_c
            wl_bf[ks, :] = wl_c
            x = x_ref[0:MOE_BIG, ks]
            pg = jnp.dot(x, wg_c, preferred_element_type=F32)
            pn = jnp.dot(x, wl_c, preferred_element_type=F32)
            gate = pg if gate is None else gate + pg
            lin = pn if lin is None else lin + pn
        finish(0, MOE_BIG, gate, lin)

    @pl.when(rows > 0)
    def _():
        _for_row_blocks(rows, block, cast, fused_first)


def _expert_down_kernel(ue_ref, uw_ref, ur_ref, x_ref, w_ref, b_ref, o_ref, w_bf):
    rows = ur_ref[pl.program_id(0)]

    def cast():
        w_bf[...] = w_ref[...].astype(BF16)

    def block(r0, size):
        x = x_ref[pl.ds(r0, size), :]
        o_ref[pl.ds(r0, size), :] = jnp.dot(x, w_bf[...], preferred_element_type=F32) + b_ref[...]

    def fused_first():
        acc = None
        for ks in _k_chunks(x_ref.shape[1]):
            w_c = w_ref[ks, :].astype(BF16)
            w_bf[ks, :] = w_c
            p = jnp.dot(x_ref[0:MOE_BIG, ks], w_c, preferred_element_type=F32)
            acc = p if acc is None else acc + p
        o_ref[0:MOE_BIG, :] = acc + b_ref[...]

    @pl.when(rows > 0)
    def _():
        _for_row_blocks(rows, block, cast, fused_first)


def _col(unit_rows, u, j, nt):
    return jnp.where(unit_rows[u] > 0, j, nt - 1)


def _out_block(unit_rows, u, j, n_units):
    used = unit_rows[u] > 0
    return jnp.where(used, u, n_units), jnp.where(used, j, 0)


def _expert_glu(x_sorted, w_gate_up, b_gate_up, unit_expert, unit_win, unit_rows, n_used):
    n_slots, d = x_sorted.shape
    n_e, _, two_h = w_gate_up.shape
    h = two_h // 2
    nt = h // MOE_TN
    n_units = unit_expert.shape[0]
    b3 = b_gate_up.reshape(n_e, 1, two_h)
    return pl.pallas_call(
        _expert_glu_kernel,
        grid_spec=pltpu.PrefetchScalarGridSpec(
            num_scalar_prefetch=3,
            grid=(n_used, nt),
            in_specs=[
                pl.BlockSpec((MOE_ROWS, d), lambda u, j, ue, uw, ur: (uw[u], 0)),
                pl.BlockSpec((None, d, MOE_TN), lambda u, j, ue, uw, ur: (ue[u], 0, _col(ur, u, j, nt))),
                pl.BlockSpec((None, d, MOE_TN), lambda u, j, ue, uw, ur: (ue[u], 0, nt + _col(ur, u, j, nt))),
                pl.BlockSpec((None, 1, MOE_TN), lambda u, j, ue, uw, ur: (ue[u], 0, _col(ur, u, j, nt))),
                pl.BlockSpec((None, 1, MOE_TN), lambda u, j, ue, uw, ur: (ue[u], 0, nt + _col(ur, u, j, nt))),
            ],
            out_specs=pl.BlockSpec((MOE_ROWS, MOE_TN), lambda u, j, ue, uw, ur: _out_block(ur, u, j, n_units)),
            scratch_shapes=[pltpu.VMEM((d, MOE_TN), BF16), pltpu.VMEM((d, MOE_TN), BF16)],
        ),
        out_shape=jax.ShapeDtypeStruct((n_slots + MOE_ROWS, h), BF16),
        compiler_params=_params("arbitrary", "arbitrary"),
        name="expert_glu",
    )(unit_expert, unit_win, unit_rows, x_sorted, w_gate_up, w_gate_up, b3, b3)


def _expert_down(act, w_down, b_down, unit_expert, unit_win, unit_rows, n_used):
    h = act.shape[1]
    n_e, _, d = w_down.shape
    n_units = unit_expert.shape[0]
    n_slots = (n_units + 1) * MOE_ROWS
    tn = MOE_TN_DOWN
    nt = d // tn
    b3 = b_down.reshape(n_e, 1, d)
    return pl.pallas_call(
        _expert_down_kernel,
        grid_spec=pltpu.PrefetchScalarGridSpec(
            num_scalar_prefetch=3,
            grid=(n_used, nt),
            in_specs=[
                pl.BlockSpec((MOE_ROWS, h), lambda u, j, ue, uw, ur: (uw[u], 0)),
                pl.BlockSpec((None, h, tn), lambda u, j, ue, uw, ur: (ue[u], 0, _col(ur, u, j, nt))),
                pl.BlockSpec((None, 1, tn), lambda u, j, ue, uw, ur: (ue[u], 0, _col(ur, u, j, nt))),
            ],
            out_specs=pl.BlockSpec((MOE_ROWS, tn), lambda u, j, ue, uw, ur: _out_block(ur, u, j, n_units)),
            scratch_shapes=[pltpu.VMEM((h, tn), BF16)],
        ),
        out_shape=jax.ShapeDtypeStruct((n_slots, d), F32),
        compiler_params=_params("arbitrary", "arbitrary"),
        name="expert_down",
    )(unit_expert, unit_win, unit_rows, act, w_down, b3)


def _combine_kernel(dest_ref, y_hbm, x_ref, r_ref, lg_ref, lb_ref, op_ref, os_ref, buf, sem,
                    *, tb, alpha, n_prompt_blocks):
    s = pl.program_id(0)
    n_steps = pl.num_programs(0)
    n_rows = tb * TOP_K

    def start(step, slot):
        base = step * n_rows

        def body(a, c):
            d = dest_ref[base + a]
            pltpu.make_async_copy(y_hbm.at[pl.ds(d, 1)], buf.at[slot, pl.ds(a, 1)], sem.at[slot]).start()
            return c
        lax.fori_loop(0, n_rows, body, 0, unroll=DMA_UNROLL)

    def wait(slot):
        pltpu.make_async_copy(y_hbm.at[pl.ds(0, n_rows)], buf.at[slot], sem.at[slot]).wait()

    @pl.when(s == 0)
    def _():
        start(0, 0)

    @pl.when(s + 1 < n_steps)
    def _():
        start(s + 1, (s + 1) % 2)

    slot = s % 2
    wait(slot)
    route = r_ref[...]
    f = jnp.zeros(x_ref.shape, F32)
    for k in range(TOP_K):
        f = f + buf[slot, k * tb:(k + 1) * tb] * route[:, ROUTE_GATE + k:ROUTE_GATE + k + 1]
    y = _ln(alpha * x_ref[...] + f, lg_ref[...], lb_ref[...])

    @pl.when(s < n_prompt_blocks)
    def _():
        op_ref[...] = y

    @pl.when(s >= n_prompt_blocks)
    def _():
        os_ref[...] = y


def _combine(dest, y_slots, x, route, ln_g, ln_b, alpha, tb, n_p):
    t, d = x.shape
    dest_flat = dest.reshape(t // tb, tb, TOP_K).transpose(0, 2, 1).reshape(-1)
    npb = n_p // tb
    prow, srow = _split_rows(npb)
    return pl.pallas_call(
        functools.partial(_combine_kernel, tb=tb, alpha=alpha, n_prompt_blocks=npb),
        grid_spec=pltpu.PrefetchScalarGridSpec(
            num_scalar_prefetch=1,
            grid=(t // tb,),
            in_specs=[
                pl.BlockSpec(memory_space=pl.ANY),
                pl.BlockSpec((tb, d), lambda s, dest: (s, 0)),
                pl.BlockSpec((tb, LANES), lambda s, dest: (s, 0)),
                pl.BlockSpec((1, d), lambda s, dest: (0, 0)),
                pl.BlockSpec((1, d), lambda s, dest: (0, 0)),
            ],
            out_specs=[
                pl.BlockSpec((tb, d), lambda s, dest: (prow(s), 0)),
                pl.BlockSpec((tb, d), lambda s, dest: (srow(s), 0)),
            ],
            scratch_shapes=[pltpu.VMEM((2, TOP_K * tb, d), F32), pltpu.SemaphoreType.DMA((2,))],
        ),
        out_shape=[
            jax.ShapeDtypeStruct((n_p, d), F32),
            jax.ShapeDtypeStruct((t - n_p, d), F32),
        ],
        compiler_params=_params("arbitrary"),
        name="combine_ln",
    )(dest_flat, y_slots, x, route, ln_g, ln_b)


def _rot_cols(w):
    half = ROPE_DIM // 2
    return jnp.concatenate([-w[..., half:], w[..., :half]], axis=-1)


def _rope_cs(pos):
    half = ROPE_DIM // 2
    inv_freq = ROPE_THETA ** (-jnp.arange(half, dtype=F32) / half)
    ang = pos.astype(F32)[:, None] * inv_freq[None, :]
    cos, sin = jnp.cos(ang), jnp.sin(ang)
    return jnp.concatenate([cos, cos, sin, sin], axis=-1)


def _kv_weight(w_uk, w_uv):
    wk = jnp.zeros((KV_IN, MLA_HEADS, QK_PAD), F32)
    wk = wk.at[:KV_LORA_RANK, :, :QK_NOPE_DIM].set(w_uk)
    eye = jnp.broadcast_to(jnp.eye(ROPE_DIM, dtype=F32)[:, None, :], (ROPE_DIM, MLA_HEADS, ROPE_DIM))
    wk = wk.at[KV_LORA_RANK:KV_LORA_RANK + ROPE_DIM, :, QK_NOPE_DIM:QK_NOPE_DIM + ROPE_DIM].set(eye)
    wv = jnp.zeros((KV_IN, MLA_HEADS, V_HEAD_DIM), F32).at[:KV_LORA_RANK].set(w_uv)
    return jnp.concatenate(
        [wk.reshape(KV_IN, MLA_HEADS * QK_PAD), wv.reshape(KV_IN, MLA_WIDTH)], axis=1).astype(BF16)


def _routing_tables(route, counts, n_tok):
    n_assign = n_tok * TOP_K
    n_units = n_assign // MOE_ROWS + N_EXPERTS
    n_slots = n_units * MOE_ROWS
    e = route[:, ROUTE_IDX:ROUTE_IDX + TOP_K].astype(jnp.int32)
    rank = route[:, ROUTE_RANK:ROUTE_RANK + TOP_K].astype(jnp.int32)
    units_per_e = (counts + MOE_ROWS - 1) // MOE_ROWS
    unit_end = jnp.cumsum(units_per_e)
    unit_start = unit_end - units_per_e
    total_units = unit_end[-1]
    dest = (unit_start[e] * MOE_ROWS + rank).reshape(-1)
    u = jnp.arange(n_units, dtype=jnp.int32)
    u_eff = jnp.minimum(u, jnp.maximum(total_units - 1, 0))
    unit_expert = jnp.minimum(jnp.searchsorted(unit_end, u_eff, side="right"), N_EXPERTS - 1).astype(jnp.int32)
    rows = jnp.clip(counts[unit_expert] - (u_eff - unit_start[unit_expert]) * MOE_ROWS, 0, MOE_ROWS)
    unit_rows = jnp.where(u < total_units, rows, 0).astype(jnp.int32)
    unit_win = u_eff.astype(jnp.int32)
    slot_tok = jnp.zeros((n_slots,), jnp.int32).at[dest].set(jnp.arange(n_assign, dtype=jnp.int32) // TOP_K)
    sub_per_unit = MOE_ROWS // MOE_SUB
    n_steps = n_assign // MOE_SUB + N_EXPERTS
    blk_per_unit = (unit_rows + MOE_SUB - 1) // MOE_SUB
    blk_end = jnp.cumsum(blk_per_unit)
    n_active = blk_end[-1]
    st = jnp.minimum(jnp.arange(n_steps, dtype=jnp.int32), jnp.maximum(n_active - 1, 0))
    st_unit = jnp.minimum(jnp.searchsorted(blk_end, st, side="right"), n_units - 1).astype(jnp.int32)
    act_blk = (st_unit * sub_per_unit + st - (blk_end - blk_per_unit)[st_unit]).astype(jnp.int32)
    return dest.astype(jnp.int32), slot_tok, act_blk, n_active.reshape(1).astype(jnp.int32), \
        unit_expert, unit_win, unit_rows, total_units.astype(jnp.int32), n_slots


def _pick_block(n, target, mult):
    best = None
    for b in range(mult, min(n, target) + 1, mult):
        if n % b == 0:
            best = b
    assert best is not None, (n, target, mult)
    return best


def kernel(x_prompt, x_sample, cache_kv_latent, cache_k_rope, w_in, q_a_norm_g, kv_a_norm_g, w_q_b, w_uk, w_uv, gmlp_norm_g, gmlp_norm_b, gmlp_w_s, gmlp_b_s, w_o, ln1_g, ln1_b, w_router, b_router, w_gate_up, b_gate_up, w_down, b_down, ln2_g, ln2_b):
    batch, seq, d_model = x_prompt.shape
    dbatch, dseq, _ = x_sample.shape
    depth, _, past, _ = cache_kv_latent.shape
    n_p, n_s = batch * seq, dbatch * dseq
    n_tok = n_p + n_s
    alpha = (2 * depth) ** 0.25
    assert n_p % GMLP_CHUNK == 0 and n_s % GMLP_CHUNK == 0 and GMLP_CHUNK % dseq == 0

    bm_big = _pick_block(n_tok, 1088, 16)
    bm_mid = _pick_block(n_tok, 512, 16)
    bm_split = _pick_block(math.gcd(n_p, n_s), 256, 16)
    tb = _pick_block(math.gcd(n_p, n_s), 128, 8)

    pos = jnp.concatenate([
        jnp.tile(jnp.arange(seq, dtype=jnp.int32), batch),
        jnp.tile(past + jnp.arange(dseq, dtype=jnp.int32), dbatch)])
    kcs = _rope_cs(pos)
    qtab = jnp.concatenate([jnp.ones((n_tok, QK_NOPE_DIM), F32), kcs], axis=-1) * ATTN_SCALE

    x_p = x_prompt.reshape(n_p, d_model)
    x_s = x_sample.reshape(n_s, d_model)

    lat_p, kr_p_all, lat_s, kr_s_all, gv_s_all = [], [], [], [], []
    for l in range(depth):
        w_kr = w_in[l][:, COL_KV:COL_KR]
        w_lat = jnp.concatenate([w_in[l][:, :COL_KR], _rot_cols(w_kr)], axis=1).astype(BF16)
        w_zu = w_in[l][:, COL_KR:COL_KR + GMLP_WIDTH].astype(BF16)
        w_zv = w_in[l][:, COL_KR + GMLP_WIDTH:].astype(BF16)
        wq = w_q_b[l]
        w_q = jnp.concatenate(
            [wq, _rot_cols(wq[..., QK_NOPE_DIM:])], axis=-1).reshape(Q_LORA_RANK, MLA_HEADS * QK_PAD).astype(BF16)
        w_kv = _kv_weight(w_uk[l], w_uv[l])
        cpos = jnp.arange(GMLP_CHUNK)
        cmask = (cpos[None, :] // CHUNK) <= (cpos[:, None] // CHUNK)
        w_sp_p = jnp.where(cmask[None], gmlp_w_s[l], 0.0)
        reps = GMLP_CHUNK // dseq
        dpos = jnp.arange(dseq)
        dmask = (dpos[None, :] // CHUNK) <= (dpos[:, None] // CHUNK)
        w_d = jnp.where(dmask[None], gmlp_w_s[l][:, :dseq, :dseq], 0.0)
        w_sp_s = jnp.einsum("ab,gij->gaibj", jnp.eye(reps, dtype=F32), w_d).reshape(
            GMLP_GROUPS, GMLP_CHUNK, GMLP_CHUNK)
        w_sets = jnp.stack([w_sp_p, w_sp_s]).astype(BF16)
        b_sets = jnp.stack([gmlp_b_s[l].T, jnp.tile(gmlp_b_s[l][:, :dseq], (1, reps)).T])
        w_o_bf = w_o[l].astype(BF16)
        w_r = jnp.zeros((d_model, LANES), F32).at[:, :N_EXPERTS].set(w_router[l])
        b_r = jnp.full((1, LANES), -jnp.inf, F32).at[0, :N_EXPERTS].set(b_router[l])

        cq, ckv, kpe, a_new, x_bf = _inproj_lat(
            x_p, x_s, w_lat, q_a_norm_g[l][None], kv_a_norm_g[l][None], kcs, bm_split)
        zu = _inproj_gelu(x_bf, w_zu, BF16, bm_big, 512)
        zv = _inproj_gelu(x_bf, w_zv, F32, bm_big, 512)
        q = _qproj(cq, w_q, qtab, bm_big)

        kv_p = _matmul(a_new, w_kv, BF16, bm_big, 1024, "kv_prompt")
        tq = min(512, seq)
        att_p = _attention(q, 0, kv_p, batch, seq, seq, seq, 0, tq, tq, 4)

        sk_valid = past + dseq
        sk = (sk_valid + LANES - 1) // LANES * LANES
        cache_a = jnp.concatenate([
            cache_kv_latent[l], cache_k_rope[l],
            jnp.zeros((dbatch, past, KV_IN - KV_LORA_RANK - ROPE_DIM), F32)], axis=-1).astype(BF16)
        a_s = jnp.concatenate([
            cache_a, a_new[n_p:].reshape(dbatch, dseq, KV_IN),
            jnp.zeros((dbatch, sk - sk_valid, KV_IN), BF16)], axis=1).reshape(dbatch * sk, KV_IN)
        wuk_t = jnp.transpose(w_uk[l], (1, 2, 0)).astype(BF16)
        wuv_h = jnp.transpose(w_uv[l], (1, 0, 2)).astype(BF16)
        att_s = _attention_latent(q, n_p, a_s, wuk_t, wuv_h, dbatch, dseq, sk, sk_valid, past)
        att = jnp.concatenate([att_p, att_s], axis=0)

        g, vn = _gmlp(zu, zv, gmlp_norm_g[l][None], gmlp_norm_b[l][None], w_sets, b_sets, n_p // GMLP_CHUNK)
        x1 = _oproj(att, g, w_o_bf[:MLA_WIDTH], w_o_bf[MLA_WIDTH:], x_p, x_s, ln1_g[l][None], ln1_b[l][None],
                    alpha, bm_split, 1024)

        route, cnt = _router(x1, w_r, b_r, bm_mid)
        counts = cnt[0, :N_EXPERTS].astype(jnp.int32)
        dest, slot_tok, act_blk, n_active, unit_expert, unit_win, unit_rows, n_used, n_slots = _routing_tables(
            route, counts, n_tok)
        x_sorted = _dispatch(x1, slot_tok, act_blk, n_active, n_slots, MOE_SUB)
        act = _expert_glu(x_sorted, w_gate_up[l], b_gate_up[l], unit_expert, unit_win, unit_rows, n_used)
        y_slots = _expert_down(act, w_down[l], b_down[l], unit_expert, unit_win, unit_rows, n_used)
        x_p, x_s = _combine(dest, y_slots, x1, route, ln2_g[l][None], ln2_b[l][None], alpha, tb, n_p)

        lat_p.append(ckv[:n_p].reshape(batch, seq, KV_LORA_RANK))
        kr_p_all.append(kpe[:n_p].reshape(batch, seq, ROPE_DIM))
        lat_s.append(ckv[n_p:].reshape(dbatch, dseq, KV_LORA_RANK))
        kr_s_all.append(kpe[n_p:].reshape(dbatch, dseq, ROPE_DIM))
        gv_s_all.append(vn[n_p:].reshape(dbatch, dseq, GMLP_GROUPS, GMLP_GROUP_DIM))

    return (
        x_p.reshape(batch, seq, d_model),
        x_s.reshape(dbatch, dseq, d_model),
        jnp.stack(lat_p), jnp.stack(kr_p_all), jnp.stack(lat_s), jnp.stack(kr_s_all), jnp.stack(gv_s_all),
    )
```
